```python
import math
import jax, jax.numpy as jnp
from jax import lax
import numpy as np

D_MODEL = 4096
BATCH = 2
SEQ = 8192
DEPTH = 1
DEC_BATCH = 8
DEC_SEQ = 2048
PAST_LEN = 128

A_HEADS = 16
A_HEAD_DIM = 128
A_WIDTH = A_HEADS * A_HEAD_DIM
DILATED_PATTERNS = ((128, 1), (512, 4), (2048, 16))
MLA_HEADS = 16
QK_NOPE = 128
QK_ROPE = 64
V_HEAD = 128
Q_LORA = 1024
KV_LORA = 512
MLA_WIDTH = MLA_HEADS * V_HEAD
ROPE_THETA = 10000.0
Q_BLOCK = 128
MIX_WIDTH = A_WIDTH + MLA_WIDTH
IN_COLS = 3 * A_WIDTH + Q_LORA + KV_LORA + QK_ROPE
SPLIT_POINTS = (A_WIDTH, 2 * A_WIDTH, 3 * A_WIDTH, 3 * A_WIDTH + Q_LORA, 3 * A_WIDTH + Q_LORA + KV_LORA)
N_EXPERTS = 16
EXPERT_FF = 2048
CAPACITY_FACTOR = 2
EPS = 1e-6

kernel_name = "hymba_longnet_mla_expert_choice_encoder"


def rms_norm(x, g):
    xf = x.astype(jnp.float32)
    y = xf * lax.rsqrt(jnp.mean(xf * xf, axis=-1, keepdims=True) + EPS)
    return (y * g.astype(jnp.float32)).astype(x.dtype)


def alibi_slopes(n_heads):
    return jnp.exp2(-8.0 * jnp.arange(1, n_heads + 1, dtype=jnp.float32) / n_heads)


def rope_tables(seq_len):
    inv = ROPE_THETA ** (-jnp.arange(0, QK_ROPE, 2, dtype=jnp.float32) / QK_ROPE)
    ang = jnp.arange(seq_len, dtype=jnp.float32)[:, None] * inv[None, :]
    return jnp.cos(ang), jnp.sin(ang)


def apply_rope(x, cos, sin):
    half = QK_ROPE // 2
    shp = (cos.shape[0],) + (1,) * (x.ndim - 3) + (half,)
    c, s = cos.reshape(shp), sin.reshape(shp)
    xf = x.astype(jnp.float32)
    x1, x2 = xf[..., :half], xf[..., half:]
    return jnp.concatenate([x1 * c - x2 * s, x2 * c + x1 * s], axis=-1).astype(x.dtype)


def dilated_window_attention(q, k, v, slopes, window, dilation):
    B, S, H, Dh = q.shape
    n_side = window // (2 * dilation)
    L = S // dilation
    blk = n_side
    nb = -(-L // blk)
    pad = nb * blk - L

    def to_sub(t):
        return t.reshape(B, L, dilation, H, Dh).transpose(0, 2, 1, 3, 4).reshape(B * dilation, L, H, Dh)

    qs = jnp.pad(to_sub(q), ((0, 0), (0, pad), (0, 0), (0, 0))).reshape(B * dilation, nb, blk, H, Dh)

    def key_blocks(t):
        tp = jnp.pad(to_sub(t), ((0, 0), (blk, pad + blk), (0, 0), (0, 0))).reshape(B * dilation, nb + 2, blk, H, Dh)
        return jnp.concatenate([tp[:, :-2], tp[:, 1:-1], tp[:, 2:]], axis=2)

    kb, vb = key_blocks(k), key_blocks(v)
    s = jnp.einsum('bnqhd,bnkhd->bnhqk', qs, kb).astype(jnp.float32)
    rel = jnp.arange(3 * blk)[None, :] - blk - jnp.arange(blk)[:, None]
    key_pos = jnp.arange(nb)[:, None, None] * blk + rel[None]
    valid = (jnp.abs(rel) <= n_side)[None] & (key_pos >= 0) & (key_pos < L)
    bias = -slopes[:, None, None] * (dilation * jnp.abs(rel)).astype(jnp.float32)[None]
    s = jnp.where(valid[None, :, None], s + bias[None, None], -jnp.inf)
    m = jnp.max(s, axis=-1, keepdims=True)
    p = jnp.exp(s - m)
    denom = jnp.sum(p, axis=-1)
    o = jnp.einsum('bnhqk,bnkhd->bnqhd', p, vb.astype(jnp.float32)) / jnp.swapaxes(denom, 2, 3)[..., None]
    lse = jnp.swapaxes(m[..., 0] + jnp.log(denom), 2, 3)

    def from_sub(t):
        tail = t.shape[3:]
        t = t.reshape(B * dilation, nb * blk, *tail)[:, :L]
        t = t.reshape(B, dilation, L, *tail)
        return jnp.moveaxis(t, 1, 2).reshape(B, S, *tail)

    return from_sub(o), from_sub(lse)


def mla_attention(q_nope, q_rope, k_nope, k_rope, v):
    B, S, H, _ = q_nope.shape
    nq = S // Q_BLOCK
    scale = (QK_NOPE + QK_ROPE) ** -0.5

    def blockify(t):
        return jnp.swapaxes(t.reshape(B, nq, Q_BLOCK, *t.shape[2:]), 0, 1)

    def attend(blk):
        qn, qr = blk
        s = jnp.einsum('bqhd,bkhd->bhqk', qn, k_nope) + jnp.einsum('bqhr,bkr->bhqk', qr, k_rope)
        p = jax.nn.softmax(s.astype(jnp.float32) * scale, axis=-1)
        return jnp.einsum('bhqk,bkhd->bqhd', p.astype(v.dtype), v)

    out = lax.map(attend, (blockify(q_nope), blockify(q_rope)))
    return jnp.swapaxes(out, 0, 1).reshape(B, S, H * V_HEAD)


def expert_choice_ffn(h, w_router, w_gate, w_up, w_down):
    B, S, D = h.shape
    N = B * S
    tokens = h.reshape(N, D)
    affinity = jax.nn.softmax((tokens @ w_router).astype(jnp.float32), axis=-1)
    cap = CAPACITY_FACTOR * N // N_EXPERTS
    gate, idx = lax.top_k(affinity.T, cap)
    xe = tokens[idx]
    hid = jax.nn.silu(jnp.einsum('ecd,edf->ecf', xe, w_gate)) * jnp.einsum('ecd,edf->ecf', xe, w_up)
    ye = jnp.einsum('ecf,efd->ecd', hid, w_down) * gate[..., None].astype(h.dtype)
    out = jnp.zeros((N, D), ye.dtype).at[idx.reshape(-1)].add(ye.reshape(-1, D))
    return out.reshape(B, S, D)


def encoder_layer(x, norm_mix_g, w_in, a_q_norm_g, a_k_norm_g, mla_q_a_norm_g, mla_w_q_b, mla_kv_a_norm_g,
                  mla_w_kv_b, mla_q_nope_norm_g, mla_q_rope_norm_g, mla_k_nope_norm_g, mla_k_rope_norm_g,
                  out_norm_a_g, out_norm_b_g, w_o, norm_ffn_g, w_router, w_gate, w_up, w_down):
    B, S, _ = x.shape
    h = rms_norm(x, norm_mix_g)
    proj = h @ w_in
    a_q, a_k, a_v, q_lat, kv_lat, k_rope = jnp.split(proj, SPLIT_POINTS, axis=-1)

    a_q = rms_norm(a_q.reshape(B, S, A_HEADS, A_HEAD_DIM), a_q_norm_g) * (A_HEAD_DIM ** -0.5)
    a_k = rms_norm(a_k.reshape(B, S, A_HEADS, A_HEAD_DIM), a_k_norm_g)
    a_v = a_v.reshape(B, S, A_HEADS, A_HEAD_DIM)
    slopes = alibi_slopes(A_HEADS)
    outs, lses = [], []
    for window, dilation in DILATED_PATTERNS:
        o_i, lse_i = dilated_window_attention(a_q, a_k, a_v, slopes, window, dilation)
        outs.append(o_i)
        lses.append(lse_i)
    wts = jax.nn.softmax(jnp.stack(lses, axis=0), axis=0)
    o_a = wts[0][..., None] * outs[0]
    for i in range(1, len(DILATED_PATTERNS)):
        o_a = o_a + wts[i][..., None] * outs[i]
    out_a = o_a.reshape(B, S, A_WIDTH).astype(x.dtype)

    c_q = rms_norm(q_lat, mla_q_a_norm_g)
    q = (c_q @ mla_w_q_b).reshape(B, S, MLA_HEADS, QK_NOPE + QK_ROPE)
    c_kv = rms_norm(kv_lat, mla_kv_a_norm_g)
    kv = (c_kv @ mla_w_kv_b).reshape(B, S, MLA_HEADS, QK_NOPE + V_HEAD)
    cos, sin = rope_tables(S)
    q_nope = rms_norm(q[..., :QK_NOPE], mla_q_nope_norm_g)
    q_rope = apply_rope(rms_norm(q[..., QK_NOPE:], mla_q_rope_norm_g), cos, sin)
    k_nope = rms_norm(kv[..., :QK_NOPE], mla_k_nope_norm_g)
    v_b = kv[..., QK_NOPE:]
    k_rope = apply_rope(rms_norm(k_rope, mla_k_rope_norm_g), cos, sin)
    out_b = mla_attention(q_nope, q_rope, k_nope, k_rope, v_b)

    mix = jnp.concatenate([rms_norm(out_a, out_norm_a_g), rms_norm(out_b, out_norm_b_g)], axis=-1)
    x = x + mix @ w_o
    x = x + expert_choice_ffn(rms_norm(x, norm_ffn_g), w_router, w_gate, w_up, w_down)
    return x


def setup_inputs(seed: int = 0) -> dict:
    key = jax.random.key(seed)
    ks = jax.random.split(key, 24)
    f32 = jnp.float32

    def nrm(k, shape, scale):
        return jax.random.normal(k, shape, f32) * scale

    def gain(k, shape):
        return 1.0 + 0.01 * jax.random.normal(k, shape, f32)

    L = DEPTH
    return {
        "x_prompt": jax.random.normal(ks[0], (BATCH, SEQ, D_MODEL), f32),
        "x_sample": jax.random.normal(ks[1], (DEC_BATCH, DEC_SEQ, D_MODEL), f32),
        "norm_mix_g": gain(ks[2], (L, D_MODEL)),
        "w_in": nrm(ks[3], (L, D_MODEL, IN_COLS), D_MODEL ** -0.5),
        "a_q_norm_g": gain(ks[4], (L, A_HEAD_DIM)),
        "a_k_norm_g": gain(ks[5], (L, A_HEAD_DIM)),
        "mla_q_a_norm_g": gain(ks[6], (L, Q_LORA)),
        "mla_w_q_b": nrm(ks[7], (L, Q_LORA, MLA_HEADS * (QK_NOPE + QK_ROPE)), Q_LORA ** -0.5),
        "mla_kv_a_norm_g": gain(ks[8], (L, KV_LORA)),
        "mla_w_kv_b": nrm(ks[9], (L, KV_LORA, MLA_HEADS * (QK_NOPE + V_HEAD)), KV_LORA ** -0.5),
        "mla_q_nope_norm_g": gain(ks[10], (L, QK_NOPE)),
        "mla_q_rope_norm_g": gain(ks[11], (L, QK_ROPE)),
        "mla_k_nope_norm_g": gain(ks[12], (L, QK_NOPE)),
        "mla_k_rope_norm_g": gain(ks[13], (L, QK_ROPE)),
        "out_norm_a_g": gain(ks[14], (L, A_WIDTH)),
        "out_norm_b_g": gain(ks[15], (L, MLA_WIDTH)),
        "w_o": nrm(ks[16], (L, MIX_WIDTH, D_MODEL), MIX_WIDTH ** -0.5),
        "norm_ffn_g": gain(ks[17], (L, D_MODEL)),
        "w_router": nrm(ks[18], (L, D_MODEL, N_EXPERTS), D_MODEL ** -0.5),
        "w_gate": nrm(ks[19], (L, N_EXPERTS, D_MODEL, EXPERT_FF), D_MODEL ** -0.5),
        "w_up": nrm(ks[20], (L, N_EXPERTS, D_MODEL, EXPERT_FF), D_MODEL ** -0.5),
        "w_down": nrm(ks[21], (L, N_EXPERTS, EXPERT_FF, D_MODEL), EXPERT_FF ** -0.5),
    }


def reference(x_prompt, x_sample, norm_mix_g, w_in, a_q_norm_g, a_k_norm_g, mla_q_a_norm_g, mla_w_q_b,
              mla_kv_a_norm_g, mla_w_kv_b, mla_q_nope_norm_g, mla_q_rope_norm_g, mla_k_nope_norm_g,
              mla_k_rope_norm_g, out_norm_a_g, out_norm_b_g, w_o, norm_ffn_g, w_router, w_gate, w_up, w_down):
    def run_trunk(x):
        for l in range(DEPTH):
            x = encoder_layer(x, norm_mix_g[l], w_in[l], a_q_norm_g[l], a_k_norm_g[l], mla_q_a_norm_g[l],
                              mla_w_q_b[l], mla_kv_a_norm_g[l], mla_w_kv_b[l], mla_q_nope_norm_g[l],
                              mla_q_rope_norm_g[l], mla_k_nope_norm_g[l], mla_k_rope_norm_g[l],
                              out_norm_a_g[l], out_norm_b_g[l], w_o[l], norm_ffn_g[l], w_router[l],
                              w_gate[l], w_up[l], w_down[l])
        return x

    y_prompt = run_trunk(x_prompt)
    y_sample = run_trunk(x_sample)
    return (y_prompt, y_sample)
```

```python
import functools
import math

import jax
import jax.numpy as jnp
from jax import lax
from jax.experimental import pallas as pl
from jax.experimental.pallas import tpu as pltpu

F32 = jnp.float32
BF16 = jnp.bfloat16

D_MODEL = 4096
A_HEADS = 16
A_HEAD_DIM = 128
A_WIDTH = A_HEADS * A_HEAD_DIM
DILATED_PATTERNS = ((128, 1), (512, 4), (2048, 16))
MLA_HEADS = 16
QK_NOPE = 128
QK_ROPE = 64
V_HEAD = 128
Q_LORA = 1024
KV_LORA = 512
MLA_WIDTH = MLA_HEADS * V_HEAD
ROPE_THETA = 10000.0
N_EXPERTS = 16
EXPERT_FF = 2048
CAPACITY_FACTOR = 2
EPS = 1e-6

LANES = 128
MLA_HEAD_PAD = 2 * LANES
N_SIDE = 64

TM_INPROJ = 512
TN_INPROJ = 512
TM_LAT = 256
TM_MLAUP = 256
TQ_MLA = 512
TK_MLA = 512
TQ_DIL = 128
TM_OUT = 256
TN_OUT = 512
TM_ROUTER = 256
TF_MOE = 256
TK_DOWN = 256
R_DOWN = 1024
G_GATHER = 256
VMEM_LIMIT = 56 * 1024 * 1024


def _cparams(sem):
    return pltpu.CompilerParams(dimension_semantics=sem, vmem_limit_bytes=VMEM_LIMIT)


def _rms(x, g):
    ms = jnp.mean(x * x, axis=-1, keepdims=True)
    return x * lax.rsqrt(ms + EPS) * g


def _dot(a, b):
    return jnp.dot(a, b, preferred_element_type=F32)


def _dot_nt(a, b):
    return lax.dot_general(a, b, (((1,), (1,)), ((), ())), preferred_element_type=F32)


def _inproj_a_body(x_ref, g_ref, w_ref, gq_ref, gk_ref, o_ref, xn_ref, *, chunk):
    j = pl.program_id(1)
    tm = x_ref.shape[0]
    tn = w_ref.shape[1]

    @pl.when(j == 0)
    def _():
        def norm_chunk(c, carry):
            r = pl.multiple_of(c * chunk, chunk)
            xn_ref[pl.ds(r, chunk), :] = _rms(x_ref[pl.ds(r, chunk), :], g_ref[...]).astype(BF16)
            return carry

        lax.fori_loop(0, tm // chunk, norm_chunk, 0)

    y = _dot(xn_ref[...], w_ref[...])
    nq = A_WIDTH // tn

    def head_norm(gain_ref, scale):
        for h in range(tn // A_HEAD_DIM):
            yh = y[:, h * A_HEAD_DIM:(h + 1) * A_HEAD_DIM]
            ms = jnp.mean(yh * yh, axis=-1, keepdims=True)
            o_ref[:, h * A_HEAD_DIM:(h + 1) * A_HEAD_DIM] = (
                yh * lax.rsqrt(ms + EPS) * (gain_ref[...] * scale)).astype(BF16)

    @pl.when(j < nq)
    def _():
        head_norm(gq_ref, A_HEAD_DIM ** -0.5)

    @pl.when((j >= nq) & (j < 2 * nq))
    def _():
        head_norm(gk_ref, 1.0)

    @pl.when(j >= 2 * nq)
    def _():
        o_ref[...] = y.astype(BF16)


def _inproj_a(x, g, w_a, gq, gk):
    n = x.shape[0]
    tm = min(TM_INPROJ, n)
    tn = TN_INPROJ
    ncol = w_a.shape[1]
    return pl.pallas_call(
        functools.partial(_inproj_a_body, chunk=128),
        grid=(n // tm, ncol // tn),
        in_specs=[
            pl.BlockSpec((tm, D_MODEL), lambda i, j: (i, 0)),
            pl.BlockSpec((1, D_MODEL), lambda i, j: (0, 0)),
            pl.BlockSpec((D_MODEL, tn), lambda i, j: (0, j)),
            pl.BlockSpec((1, A_HEAD_DIM), lambda i, j: (0, 0)),
            pl.BlockSpec((1, A_HEAD_DIM), lambda i, j: (0, 0)),
        ],
        out_specs=pl.BlockSpec((tm, tn), lambda i, j: (i, j)),
        out_shape=jax.ShapeDtypeStruct((n, ncol), BF16),
        scratch_shapes=[pltpu.VMEM((tm, D_MODEL), BF16)],
        compiler_params=_cparams(("parallel", "arbitrary")),
    )(x, g, w_a, gq, gk)


def _rope_padded(t, cos, sin):
    return t * cos + pltpu.roll(t, 2 * (QK_ROPE // 2), 1) * sin


def _inproj_lat_body(x_ref, g_ref, wq_ref, wkv_ref, wkr_ref, gq_ref, gkv_ref, gkr_ref, cos_ref, sin_ref,
                     cq_ref, ckv_ref, kr_ref):
    xn = _rms(x_ref[...], g_ref[...]).astype(BF16)
    cq_ref[...] = _rms(_dot(xn, wq_ref[...]), gq_ref[...]).astype(BF16)
    ckv_ref[...] = _rms(_dot(xn, wkv_ref[...]), gkv_ref[...]).astype(BF16)
    kr = _dot(xn, wkr_ref[...])
    ms = jnp.sum(kr * kr, axis=-1, keepdims=True) * (1.0 / QK_ROPE)
    krn = kr * lax.rsqrt(ms + EPS) * gkr_ref[...]
    kr_ref[...] = _rope_padded(krn, cos_ref[...], sin_ref[...]).astype(BF16)


def _inproj_lat(x, g, wq, wkv, wkr, gq, gkv, gkr, cos_pad, sin_pad, seq):
    n = x.shape[0]
    tm = min(TM_LAT, seq)
    nseq = seq // tm
    const = lambda i: (0, 0)
    return pl.pallas_call(
        _inproj_lat_body,
        grid=(n // tm,),
        in_specs=[
            pl.BlockSpec((tm, D_MODEL), lambda i: (i, 0)),
            pl.BlockSpec((1, D_MODEL), const),
            pl.BlockSpec((D_MODEL, Q_LORA), const),
            pl.BlockSpec((D_MODEL, KV_LORA), const),
            pl.BlockSpec((D_MODEL, LANES), const),
            pl.BlockSpec((1, Q_LORA), const),
            pl.BlockSpec((1, KV_LORA), const),
            pl.BlockSpec((1, LANES), const),
            pl.BlockSpec((tm, LANES), lambda i: (i % nseq, 0)),
            pl.BlockSpec((tm, LANES), lambda i: (i % nseq, 0)),
        ],
        out_specs=[
            pl.BlockSpec((tm, Q_LORA), lambda i: (i, 0)),
            pl.BlockSpec((tm, KV_LORA), lambda i: (i, 0)),
            pl.BlockSpec((tm, LANES), lambda i: (i, 0)),
        ],
        out_shape=[
            jax.ShapeDtypeStruct((n, Q_LORA), BF16),
            jax.ShapeDtypeStruct((n, KV_LORA), BF16),
            jax.ShapeDtypeStruct((n, LANES), BF16),
        ],
        compiler_params=_cparams(("parallel",)),
    )(x, g, wq, wkv, wkr, gq, gkv, gkr, cos_pad, sin_pad)


def _mla_up_body(cq_ref, ckv_ref, kr_ref, wq_ref, wkv_ref, gqn_ref, gqr_ref, gkn_ref, cos_ref, sin_ref,
                 q_ref, k_ref, v_ref):
    cq = cq_ref[...]
    ckv = ckv_ref[...]
    kr = kr_ref[...]
    cos = cos_ref[...]
    sin = sin_ref[...]
    scale = (QK_NOPE + QK_ROPE) ** -0.5
    for h in range(MLA_HEADS):
        lo = h * MLA_HEAD_PAD
        q = _dot(cq, wq_ref[:, lo:lo + MLA_HEAD_PAD])
        qn = _rms(q[:, :LANES], gqn_ref[...] * scale)
        qr = q[:, LANES:]
        ms = jnp.sum(qr * qr, axis=-1, keepdims=True) * (1.0 / QK_ROPE)
        qr = _rope_padded(qr * lax.rsqrt(ms + EPS) * (gqr_ref[...] * scale), cos, sin)
        q_ref[:, lo:lo + LANES] = qn.astype(BF16)
        q_ref[:, lo + LANES:lo + MLA_HEAD_PAD] = qr.astype(BF16)
        kv = _dot(ckv, wkv_ref[:, lo:lo + MLA_HEAD_PAD])
        k_ref[:, lo:lo + LANES] = _rms(kv[:, :LANES], gkn_ref[...]).astype(BF16)
        k_ref[:, lo + LANES:lo + MLA_HEAD_PAD] = kr
        v_ref[:, h * V_HEAD:(h + 1) * V_HEAD] = kv[:, LANES:].astype(BF16)


def _mla_up(cq, ckv, kr, wq_pad, wkv, gqn, gqr_pad, gkn, cos_pad, sin_pad, seq):
    n = cq.shape[0]
    tm = min(TM_MLAUP, seq)
    nseq = seq // tm
    const = lambda i: (0, 0)
    row = lambda i: (i, 0)
    wide = MLA_HEADS * MLA_HEAD_PAD
    return pl.pallas_call(
        _mla_up_body,
        grid=(n // tm,),
        in_specs=[
            pl.BlockSpec((tm, Q_LORA), row),
            pl.BlockSpec((tm, KV_LORA), row),
            pl.BlockSpec((tm, LANES), row),
            pl.BlockSpec((Q_LORA, wide), const),
            pl.BlockSpec((KV_LORA, wide), const),
            pl.BlockSpec((1, LANES), const),
            pl.BlockSpec((1, LANES), const),
            pl.BlockSpec((1, LANES), const),
            pl.BlockSpec((tm, LANES), lambda i: (i % nseq, 0)),
            pl.BlockSpec((tm, LANES), lambda i: (i % nseq, 0)),
        ],
        out_specs=[
            pl.BlockSpec((tm, wide), row),
            pl.BlockSpec((tm, wide), row),
            pl.BlockSpec((tm, MLA_WIDTH), row),
        ],
        out_shape=[
            jax.ShapeDtypeStruct((n, wide), BF16),
            jax.ShapeDtypeStruct((n, wide), BF16),
            jax.ShapeDtypeStruct((n, MLA_WIDTH), BF16),
        ],
        compiler_params=_cparams(("parallel",)),
    )(cq, ckv, kr, wq_pad, wkv, gqn, gqr_pad, gkn, cos_pad, sin_pad)


def _mla_flash_body(q_ref, k_ref, v_ref, o_ref, *, tk):
    q = q_ref[...]
    tq = q.shape[0]
    nk = k_ref.shape[0] // tk

    def step(j, carry):
        m, l, acc = carry
        r = pl.multiple_of(j * tk, tk)
        s = _dot_nt(q, k_ref[pl.ds(r, tk), :])
        m_new = jnp.maximum(m, jnp.max(s, axis=-1, keepdims=True))
        alpha = jnp.exp(m - m_new)
        p = jnp.exp(s - m_new)
        l = alpha * l + jnp.sum(p, axis=-1, keepdims=True)
        acc = alpha * acc + _dot(p.astype(BF16), v_ref[pl.ds(r, tk), :])
        return m_new, l, acc

    init = (jnp.full((tq, 1), -jnp.inf, F32), jnp.zeros((tq, 1), F32), jnp.zeros((tq, V_HEAD), F32))
    _, l, acc = lax.fori_loop(0, nk, step, init)
    o_ref[...] = acc / l


def _mla_flash(qm, km, vm):
    b, s, _ = qm.shape
    tq = min(TQ_MLA, s)
    tk = min(TK_MLA, s)
    return pl.pallas_call(
        functools.partial(_mla_flash_body, tk=tk),
        grid=(b, MLA_HEADS, s // tq),
        in_specs=[
            pl.BlockSpec((None, tq, MLA_HEAD_PAD), lambda bi, h, i: (bi, i, h)),
            pl.BlockSpec((None, s, MLA_HEAD_PAD), lambda bi, h, i: (bi, 0, h)),
            pl.BlockSpec((None, s, V_HEAD), lambda bi, h, i: (bi, 0, h)),
        ],
        out_specs=pl.BlockSpec((None, tq, V_HEAD), lambda bi, h, i: (bi, i, h)),
        out_shape=jax.ShapeDtypeStruct((b, s, MLA_WIDTH), F32),
        compiler_params=_cparams(("parallel", "parallel", "arbitrary")),
    )(qm, km, vm)


def _dilated_body(q_ref, k0_ref, kp_ref, kn_ref, v0_ref, vp_ref, vn_ref, o_ref, lse_ref, kw_ref, vw_ref,
                  *, dil, nq):
    i = pl.program_id(2)
    tq = q_ref.shape[0]
    win = tq + 2 * N_SIDE
    kw_ref[0:N_SIDE, :] = kp_ref[...]
    kw_ref[N_SIDE:N_SIDE + tq, :] = k0_ref[...]
    kw_ref[N_SIDE + tq:win, :] = kn_ref[...]
    vw_ref[0:N_SIDE, :] = vp_ref[...]
    vw_ref[N_SIDE:N_SIDE + tq, :] = v0_ref[...]
    vw_ref[N_SIDE + tq:win, :] = vn_ref[...]

    @pl.when(i == nq - 1)
    def _():
        kw_ref[N_SIDE + tq:win, :] = jnp.zeros((N_SIDE, A_WIDTH), BF16)
        vw_ref[N_SIDE + tq:win, :] = jnp.zeros((N_SIDE, A_WIDTH), BF16)

    row = lax.broadcasted_iota(jnp.int32, (tq, win), 0)
    col = lax.broadcasted_iota(jnp.int32, (tq, win), 1)
    rel = col - N_SIDE - row
    arel = jnp.abs(rel)
    lower_ok = (rel >= 0) | (row >= N_SIDE) | (i > 0)
    upper_ok = (rel < N_SIDE) | (row < tq - N_SIDE) | (i < nq - 1)
    valid = (arel <= N_SIDE) & lower_ok & upper_ok
    dist = jnp.where(valid, (dil * arel).astype(F32), jnp.inf)

    lane = lax.broadcasted_iota(jnp.int32, (tq, LANES), 1)
    lse_all = jnp.zeros((tq, LANES), F32)
    for h in range(A_HEADS):
        sl = slice(h * A_HEAD_DIM, (h + 1) * A_HEAD_DIM)
        slope = 2.0 ** (-8.0 * (h + 1) / A_HEADS)
        s = _dot_nt(q_ref[:, sl], kw_ref[:, sl]) - slope * dist
        m = jnp.max(s, axis=-1, keepdims=True)
        p = jnp.exp(s - m)
        l = jnp.sum(p, axis=-1, keepdims=True)
        o_ref[:, sl] = _dot(p.astype(BF16), vw_ref[:, sl]) / l
        lse_all = jnp.where(lane == h, m + jnp.log(l), lse_all)
    lse_ref[...] = lse_all


def _dilated_branch(qkv, dil):
    b, s, _ = qkv.shape
    sub = s // dil
    tq = TQ_DIL
    nq = sub // tq
    nhalo = sub // N_SIDE
    per = tq // N_SIDE
    view = qkv.reshape(b, sub, dil * 3 * A_WIDTH)

    def own(which):
        return pl.BlockSpec((None, tq, A_WIDTH), lambda bi, r, i: (bi, i, 3 * r + which))

    def prev(which):
        return pl.BlockSpec((None, N_SIDE, A_WIDTH),
                            lambda bi, r, i: (bi, jnp.maximum(per * i - 1, 0), 3 * r + which))

    def nxt(which):
        return pl.BlockSpec((None, N_SIDE, A_WIDTH),
                            lambda bi, r, i: (bi, jnp.minimum(per * (i + 1), nhalo - 1), 3 * r + which))

    o, lse = pl.pallas_call(
        functools.partial(_dilated_body, dil=dil, nq=nq),
        grid=(b, dil, nq),
        in_specs=[own(0), own(1), prev(1), nxt(1), own(2), prev(2), nxt(2)],
        out_specs=[
            pl.BlockSpec((None, tq, A_WIDTH), lambda bi, r, i: (bi, i, r)),
            pl.BlockSpec((None, tq, LANES), lambda bi, r, i: (bi, i, r)),
        ],
        out_shape=[
            jax.ShapeDtypeStruct((b, sub, dil * A_WIDTH), F32),
            jax.ShapeDtypeStruct((b, sub, dil * LANES), F32),
        ],
        scratch_shapes=[
            pltpu.VMEM((tq + 2 * N_SIDE, A_WIDTH), BF16),
            pltpu.VMEM((tq + 2 * N_SIDE, A_WIDTH), BF16),
        ],
        compiler_params=_cparams(("parallel", "parallel", "arbitrary")),
    )(view, view, view, view, view, view, view)
    return o.reshape(b * s, A_WIDTH), lse.reshape(b * s, LANES)


def _outproj_body(o1_ref, o2_ref, o3_ref, l1_ref, l2_ref, l3_ref, ob_ref, ga_ref, gb_ref, w_ref, x_ref,
                  y_ref, mix_ref, oa_ref):
    j = pl.program_id(1)

    @pl.when(j == 0)
    def _():
        l1, l2, l3 = l1_ref[...], l2_ref[...], l3_ref[...]
        mx = jnp.maximum(jnp.maximum(l1, l2), l3)
        e1, e2, e3 = jnp.exp(l1 - mx), jnp.exp(l2 - mx), jnp.exp(l3 - mx)
        inv = 1.0 / (e1 + e2 + e3)
        w1, w2, w3 = e1 * inv, e2 * inv, e3 * inv
        ss = jnp.zeros((l1.shape[0], 1), F32)
        for h in range(A_HEADS):
            sl = slice(h * A_HEAD_DIM, (h + 1) * A_HEAD_DIM)
            oa = (w1[:, h:h + 1] * o1_ref[:, sl] + w2[:, h:h + 1] * o2_ref[:, sl]
                  + w3[:, h:h + 1] * o3_ref[:, sl])
            oa_ref[:, sl] = oa
            ss = ss + jnp.sum(oa * oa, axis=-1, keepdims=True)
        inv_a = lax.rsqrt(ss * (1.0 / A_WIDTH) + EPS)
        mix_ref[:, :A_WIDTH] = (oa_ref[...] * inv_a * ga_ref[...]).astype(BF16)
        mix_ref[:, A_WIDTH:] = _rms(ob_ref[...], gb_ref[...]).astype(BF16)

    y_ref[...] = x_ref[...] + _dot(mix_ref[...], w_ref[...])


def _outproj(o1, o2, o3, l1, l2, l3, ob, ga, gb, w_o, x):
    n = x.shape[0]
    tm = min(TM_OUT, n)
    tn = TN_OUT
    row = lambda i, j: (i, 0)
    const = lambda i, j: (0, 0)
    mixw = A_WIDTH + MLA_WIDTH
    return pl.pallas_call(
        _outproj_body,
        grid=(n // tm, D_MODEL // tn),
        in_specs=[
            pl.BlockSpec((tm, A_WIDTH), row), pl.BlockSpec((tm, A_WIDTH), row), pl.BlockSpec((tm, A_WIDTH), row),
            pl.BlockSpec((tm, LANES), row), pl.BlockSpec((tm, LANES), row), pl.BlockSpec((tm, LANES), row),
            pl.BlockSpec((tm, MLA_WIDTH), row),
            pl.BlockSpec((1, A_WIDTH), const), pl.BlockSpec((1, MLA_WIDTH), const),
            pl.BlockSpec((mixw, tn), lambda i, j: (0, j)),
            pl.BlockSpec((tm, tn), lambda i, j: (i, j)),
        ],
        out_specs=pl.BlockSpec((tm, tn), lambda i, j: (i, j)),
        out_shape=jax.ShapeDtypeStruct((n, D_MODEL), F32),
        scratch_shapes=[pltpu.VMEM((tm, mixw), BF16), pltpu.VMEM((tm, A_WIDTH), F32)],
        compiler_params=_cparams(("parallel", "arbitrary")),
    )(o1, o2, o3, l1, l2, l3, ob, ga, gb, w_o, x)


def _router_body(x_ref, g_ref, wr_ref, h_ref, aff_ref):
    h = _rms(x_ref[...], g_ref[...])
    h_ref[...] = h
    logits = lax.dot_general(wr_ref[...], h, (((1,), (1,)), ((), ())), preferred_element_type=F32,
                             precision=lax.Precision.HIGHEST)
    m = jnp.max(logits, axis=0, keepdims=True)
    e = jnp.exp(logits - m)
    aff_ref[...] = e / jnp.sum(e, axis=0, keepdims=True)


def _router(x2, g, wr_t):
    n = x2.shape[0]
    tm = min(TM_ROUTER, n)
    return pl.pallas_call(
        _router_body,
        grid=(n // tm,),
        in_specs=[
            pl.BlockSpec((tm, D_MODEL), lambda i: (i, 0)),
            pl.BlockSpec((1, D_MODEL), lambda i: (0, 0)),
            pl.BlockSpec((N_EXPERTS, D_MODEL), lambda i: (0, 0)),
        ],
        out_specs=[
            pl.BlockSpec((tm, D_MODEL), lambda i: (i, 0)),
            pl.BlockSpec((N_EXPERTS, tm), lambda i: (0, i)),
        ],
        out_shape=[
            jax.ShapeDtypeStruct((n, D_MODEL), F32),
            jax.ShapeDtypeStruct((N_EXPERTS, n), F32),
        ],
        compiler_params=_cparams(("parallel",)),
    )(x2, g, wr_t)


def _sum_all(x):
    return jnp.sum(jnp.sum(x, axis=1, keepdims=True), axis=0, keepdims=True)


def _topk_body(a_ref, idx_ref, gate_ref, *, cap):
    a = a_ref[...]
    nc = a.shape[0]
    bits = pltpu.bitcast(a, jnp.int32)
    thr = jnp.zeros((1, 1), jnp.int32)
    for bit in range(30, -1, -1):
        cand = thr + (1 << bit)
        cnt = _sum_all((bits >= cand).astype(F32))
        thr = jnp.where(cnt >= cap, cand, thr)
    gt = bits > thr
    eq = bits == thr
    eqf = eq.astype(F32)
    need = cap - _sum_all(gt.astype(F32))

    li = lax.broadcasted_iota(jnp.int32, (LANES, LANES), 0)
    lj = lax.broadcasted_iota(jnp.int32, (LANES, LANES), 1)
    upper_incl = (li <= lj).astype(BF16)
    ci = lax.broadcasted_iota(jnp.int32, (nc, nc), 0)
    cj = lax.broadcasted_iota(jnp.int32, (nc, nc), 1)
    lower_strict = (cj < ci).astype(BF16)
    upper_incl_c = (ci <= cj).astype(BF16)

    p_eq = _dot(eqf.astype(BF16), upper_incl)
    tot_eq = jnp.broadcast_to(p_eq[:, LANES - 1:LANES], (nc, LANES))
    off_eq = _dot(lower_strict, tot_eq.astype(BF16))
    sel = gt | (eq & ((p_eq - eqf + off_eq) < need))
    selb = sel.astype(F32).astype(BF16)

    p_sel = _dot(selb, upper_incl)
    tot_row = _dot_nt(jnp.ones((8, LANES), BF16), selb)
    offi_row = _dot(tot_row.astype(BF16), upper_incl_c)[0:1, :]
    offe_row = offi_row - tot_row[0:1, :]

    slot_c = lax.broadcasted_iota(jnp.int32, (cap, nc), 0).astype(F32)
    chunk_of = jnp.sum((offi_row <= slot_c).astype(F32), axis=1, keepdims=True)
    onehot = lax.broadcasted_iota(jnp.int32, (cap, nc), 1).astype(F32) == chunk_of
    onehot_b = onehot.astype(F32).astype(BF16)
    run = _dot(onehot_b, p_sel.astype(BF16))
    offe = jnp.sum(jnp.where(onehot, offe_row, 0.0), axis=1, keepdims=True)
    slot_l = lax.broadcasted_iota(jnp.int32, (cap, LANES), 0).astype(F32)
    lane_of = jnp.sum(((run + offe) <= slot_l).astype(F32), axis=1, keepdims=True)
    idx_ref[...] = (chunk_of * LANES + lane_of).astype(jnp.int32)

    a1 = a.astype(BF16)
    r1 = a - a1.astype(F32)
    a2 = r1.astype(BF16)
    a3 = (r1 - a2.astype(F32)).astype(BF16)
    rows = _dot(onehot_b, a1) + _dot(onehot_b, a2) + _dot(onehot_b, a3)
    lanes = lax.broadcasted_iota(jnp.int32, (cap, LANES), 1).astype(F32)
    gate_ref[...] = jnp.sum(jnp.where(lanes == lane_of, rows, 0.0), axis=1, keepdims=True)


def _topk(aff_t, cap):
    e, n = aff_t.shape
    nc = n // LANES
    return pl.pallas_call(
        functools.partial(_topk_body, cap=cap),
        grid=(e,),
        in_specs=[pl.BlockSpec((None, nc, LANES), lambda i: (i, 0, 0))],
        out_specs=[
            pl.BlockSpec((None, cap, 1), lambda i: (i, 0, 0)),
            pl.BlockSpec((None, cap, 1), lambda i: (i, 0, 0)),
        ],
        out_shape=[
            jax.ShapeDtypeStruct((e, cap, 1), jnp.int32),
            jax.ShapeDtypeStruct((e, cap, 1), F32),
        ],
        compiler_params=_cparams(("parallel",)),
    )(aff_t.reshape(e, nc, LANES))


def _row_copy(src_hbm, tok, dst_ref, r, sem):
    return pltpu.make_async_copy(src_hbm.at[pl.ds(tok, 1), :], dst_ref.at[pl.ds(r, 1), :], sem)


def _moe_up_body(idx_ref, h_hbm, gate_ref, wg_ref, wu_ref, hid_ref, xe_ref, stage_ref, sem, *, cap, rows):
    e = pl.program_id(0)
    f = pl.program_id(1)

    @pl.when(f == 0)
    def _():
        nchunk = cap // rows

        def start(c, slot):
            def body(r, carry):
                tok = idx_ref[e * cap + c * rows + r]
                _row_copy(h_hbm, tok, stage_ref.at[slot], r, sem.at[slot]).start()
                return carry
            lax.fori_loop(0, rows, body, 0)

        def wait(slot):
            def body(r, carry):
                _row_copy(h_hbm, 0, stage_ref.at[slot], r, sem.at[slot]).wait()
                return carry
            lax.fori_loop(0, rows, body, 0)

        start(0, 0)
        for c in range(nchunk):
            slot = c % 2
            if c + 1 < nchunk:
                start(c + 1, 1 - slot)
            wait(slot)
            xe_ref[c * rows:(c + 1) * rows, :] = stage_ref[slot].astype(BF16)

    xe = xe_ref[...]
    g = _dot(xe, wg_ref[...].astype(BF16))
    u = _dot(xe, wu_ref[...].astype(BF16))
    hid = g / (1.0 + jnp.exp(-g)) * u * gate_ref[...]
    hid_ref[...] = hid.astype(BF16)


def _moe_up(idx_flat, h2, gate, w_gate, w_up, cap):
    n = h2.shape[0]
    tf = TF_MOE
    rows = min(G_GATHER, cap)
    grid_spec = pltpu.PrefetchScalarGridSpec(
        num_scalar_prefetch=1,
        grid=(N_EXPERTS, EXPERT_FF // tf),
        in_specs=[
            pl.BlockSpec(memory_space=pl.ANY),
            pl.BlockSpec((None, cap, 1), lambda e, f, idx: (e, 0, 0)),
            pl.BlockSpec((None, D_MODEL, tf), lambda e, f, idx: (e, 0, f)),
            pl.BlockSpec((None, D_MODEL, tf), lambda e, f, idx: (e, 0, f)),
        ],
        out_specs=pl.BlockSpec((None, cap, tf), lambda e, f, idx: (e, 0, f)),
        scratch_shapes=[
            pltpu.VMEM((cap, D_MODEL), BF16),
            pltpu.VMEM((2, rows, D_MODEL), F32),
            pltpu.SemaphoreType.DMA((2,)),
        ],
    )
    return pl.pallas_call(
        functools.partial(_moe_up_body, cap=cap, rows=rows),
        grid_spec=grid_spec,
        out_shape=jax.ShapeDtypeStruct((N_EXPERTS, cap, EXPERT_FF), BF16),
        compiler_params=_cparams(("arbitrary", "arbitrary")),
    )(idx_flat, h2, gate, w_gate, w_up)


def _moe_down_body(idx_ref, hid_ref, wd_ref, x_hbm, out_hbm, acc_ref, sem, *, cap, rows):
    del x_hbm
    e = pl.program_id(0)
    rb = pl.program_id(1)
    kt = pl.program_id(2)
    base = e * cap + rb * rows

    @pl.when(kt == 0)
    def _():
        def start(r, carry):
            pltpu.make_async_copy(out_hbm.at[pl.ds(idx_ref[base + r], 1), :], acc_ref.at[pl.ds(r, 1), :],
                                  sem.at[0]).start()
            return carry
        lax.fori_loop(0, rows, start, 0)

        def wait(r, carry):
            pltpu.make_async_copy(out_hbm.at[pl.ds(0, 1), :], acc_ref.at[pl.ds(r, 1), :], sem.at[0]).wait()
            return carry
        lax.fori_loop(0, rows, wait, 0)

    acc_ref[...] += _dot(hid_ref[...], wd_ref[...].astype(BF16))

    @pl.when(kt == pl.num_programs(2) - 1)
    def _():
        def start(r, carry):
            pltpu.make_async_copy(acc_ref.at[pl.ds(r, 1), :], out_hbm.at[pl.ds(idx_ref[base + r], 1), :],
                                  sem.at[1]).start()
            return carry
        lax.fori_loop(0, rows, start, 0)

        def wait(r, carry):
            pltpu.make_async_copy(acc_ref.at[pl.ds(r, 1), :], out_hbm.at[pl.ds(0, 1), :], sem.at[1]).wait()
            return carry
        lax.fori_loop(0, rows, wait, 0)


def _moe_down(idx_flat, hid, w_down, x2, cap):
    n = x2.shape[0]
    rows = min(R_DOWN, cap)
    tk = TK_DOWN
    grid_spec = pltpu.PrefetchScalarGridSpec(
        num_scalar_prefetch=1,
        grid=(N_EXPERTS, cap // rows, EXPERT_FF // tk),
        in_specs=[
            pl.BlockSpec((None, rows, tk), lambda e, rb, kt, idx: (e, rb, kt)),
            pl.BlockSpec((None, tk, D_MODEL), lambda e, rb, kt, idx: (e, kt, 0)),
            pl.BlockSpec(memory_space=pl.ANY),
        ],
        out_specs=pl.BlockSpec(memory_space=pl.ANY),
        scratch_shapes=[
            pltpu.VMEM((rows, D_MODEL), F32),
            pltpu.SemaphoreType.DMA((2,)),
        ],
    )
    return pl.pallas_call(
        functools.partial(_moe_down_body, cap=cap, rows=rows),
        grid_spec=grid_spec,
        out_shape=jax.ShapeDtypeStruct((n, D_MODEL), F32),
        input_output_aliases={3: 0},
        compiler_params=_cparams(("arbitrary", "arbitrary", "arbitrary")),
    )(idx_flat, hid, w_down, x2)


def _pad_rope_cols(t):
    half = QK_ROPE // 2
    z = jnp.zeros(t.shape[:-1] + (half,), t.dtype)
    return jnp.concatenate([t[..., :half], z, t[..., half:], z], axis=-1)


def _rope_tables_padded(seq):
    inv = ROPE_THETA ** (-jnp.arange(0, QK_ROPE, 2, dtype=F32) / QK_ROPE)
    ang = jnp.arange(seq, dtype=F32)[:, None] * inv[None, :]
    cos, sin = jnp.cos(ang), jnp.sin(ang)
    z = jnp.zeros_like(cos)
    return jnp.concatenate([cos, z, cos, z], axis=-1), jnp.concatenate([-sin, z, sin, z], axis=-1)


def _prepare_weights(w_in, mla_w_q_b, mla_w_kv_b, w_o, w_router, mla_q_rope_norm_g, mla_k_rope_norm_g):
    qkv_cols = 3 * A_WIDTH
    w_a = w_in[:, :qkv_cols].astype(BF16)
    w_ql = w_in[:, qkv_cols:qkv_cols + Q_LORA].astype(BF16)
    w_kvl = w_in[:, qkv_cols + Q_LORA:qkv_cols + Q_LORA + KV_LORA].astype(BF16)
    w_kr = _pad_rope_cols(w_in[:, qkv_cols + Q_LORA + KV_LORA:]).astype(BF16)
    wq = mla_w_q_b.reshape(Q_LORA, MLA_HEADS, QK_NOPE + QK_ROPE)
    wq_pad = jnp.concatenate([wq[..., :QK_NOPE], _pad_rope_cols(wq[..., QK_NOPE:])], axis=-1)
    wq_pad = wq_pad.reshape(Q_LORA, MLA_HEADS * MLA_HEAD_PAD).astype(BF16)
    return dict(
        w_a=w_a, w_ql=w_ql, w_kvl=w_kvl, w_kr=w_kr, wq_pad=wq_pad,
        wkv=mla_w_kv_b.astype(BF16), w_o=w_o.astype(BF16), wr_t=w_router.T,
        gqr_pad=_pad_rope_cols(mla_q_rope_norm_g)[None, :], gkr_pad=_pad_rope_cols(mla_k_rope_norm_g)[None, :],
    )


def _encoder_layer(x3, pw, norm_mix_g, a_q_norm_g, a_k_norm_g, mla_q_a_norm_g, mla_kv_a_norm_g,
                   mla_q_nope_norm_g, mla_k_nope_norm_g, out_norm_a_g, out_norm_b_g, norm_ffn_g,
                   w_gate, w_up, w_down):
    b, s, d = x3.shape
    n = b * s
    x = x3.reshape(n, d)
    row = lambda v: v[None, :]
    cos_pad, sin_pad = _rope_tables_padded(s)

    qkv = _inproj_a(x, row(norm_mix_g), pw["w_a"], row(a_q_norm_g), row(a_k_norm_g))
    qkv3 = qkv.reshape(b, s, 3 * A_WIDTH)
    branches = [_dilated_branch(qkv3, dil) for _, dil in DILATED_PATTERNS]

    cq, ckv, kr = _inproj_lat(x, row(norm_mix_g), pw["w_ql"], pw["w_kvl"], pw["w_kr"], row(mla_q_a_norm_g),
                              row(mla_kv_a_norm_g), pw["gkr_pad"], cos_pad, sin_pad, s)
    qm, km, vm = _mla_up(cq, ckv, kr, pw["wq_pad"], pw["wkv"], row(mla_q_nope_norm_g), pw["gqr_pad"],
                         row(mla_k_nope_norm_g), cos_pad, sin_pad, s)
    wide = MLA_HEADS * MLA_HEAD_PAD
    ob = _mla_flash(qm.reshape(b, s, wide), km.reshape(b, s, wide), vm.reshape(b, s, MLA_WIDTH))

    (o1, l1), (o2, l2), (o3, l3) = branches
    x2 = _outproj(o1, o2, o3, l1, l2, l3, ob.reshape(n, MLA_WIDTH), row(out_norm_a_g), row(out_norm_b_g),
                  pw["w_o"], x)

    h2, aff_t = _router(x2, row(norm_ffn_g), pw["wr_t"])
    cap = CAPACITY_FACTOR * n // N_EXPERTS
    idx, gate = _topk(aff_t, cap)
    idx_flat = idx.reshape(N_EXPERTS * cap)
    hid = _moe_up(idx_flat, h2, gate, w_gate, w_up, cap)
    y = _moe_down(idx_flat, hid, w_down, x2, cap)
    return y.reshape(b, s, d)


def kernel(x_prompt, x_sample, norm_mix_g, w_in, a_q_norm_g, a_k_norm_g, mla_q_a_norm_g, mla_w_q_b, mla_kv_a_norm_g, mla_w_kv_b, mla_q_nope_norm_g, mla_q_rope_norm_g, mla_k_nope_norm_g, mla_k_rope_norm_g, out_norm_a_g, out_norm_b_g, w_o, norm_ffn_g, w_router, w_gate, w_up, w_down):
    depth = w_in.shape[0]
    prepared = [_prepare_weights(w_in[l], mla_w_q_b[l], mla_w_kv_b[l], w_o[l], w_router[l],
                                 mla_q_rope_norm_g[l], mla_k_rope_norm_g[l]) for l in range(depth)]

    def run_trunk(x):
        for l in range(depth):
            pw = prepared[l]
            x = _encoder_layer(x, pw, norm_mix_g[l], a_q_norm_g[l], a_k_norm_g[l], mla_q_a_norm_g[l],
                               mla_kv_a_norm_g[l], mla_q_nope_norm_g[l], mla_k_nope_norm_g[l],
                               out_norm_a_g[l], out_norm_b_g[l], norm_ffn_g[l], w_gate[l], w_up[l], w_down[l])
        return x

    return run_trunk(x_prompt), run_trunk(x_sample)
```

```python
import functools
import math

import jax
import jax.numpy as jnp
from jax import lax
from jax.experimental import pallas as pl
from jax.experimental.pallas import tpu as pltpu

F32 = jnp.float32
BF16 = jnp.bfloat16

D_MODEL = 4096
A_HEADS = 16
A_HEAD_DIM = 128
A_WIDTH = A_HEADS * A_HEAD_DIM
DILATED_PATTERNS = ((128, 1), (512, 4), (2048, 16))
MLA_HEADS = 16
QK_NOPE = 128
QK_ROPE = 64
V_HEAD = 128
Q_LORA = 1024
KV_LORA = 512
MLA_WIDTH = MLA_HEADS * V_HEAD
ROPE_THETA = 10000.0
N_EXPERTS = 16
EXPERT_FF = 2048
CAPACITY_FACTOR = 2
EPS = 1e-6

LANES = 128
MLA_HEAD_PAD = 2 * LANES
N_SIDE = 64

TM_INPROJ = 512
TN_INPROJ = 512
TM_LAT = 256
TM_MLAUP = 256
TQ_MLA = 512
TK_MLA = 2048
TQ_DIL = 128
TM_OUT = 512
TN_OUT = 512
TM_ROUTER = 256
TF_MOE = 256
TK_DOWN = 256
R_DOWN = 1024
G_GATHER = 256
DMA_UNROLL = 8
VMEM_LIMIT = 56 * 1024 * 1024


def _cparams(sem):
    return pltpu.CompilerParams(dimension_semantics=sem, vmem_limit_bytes=VMEM_LIMIT)


def _rms(x, g):
    ms = jnp.mean(x * x, axis=-1, keepdims=True)
    return x * lax.rsqrt(ms + EPS) * g


def _dot(a, b):
    return jnp.dot(a, b, preferred_element_type=F32)


def _dot_nt(a, b):
    return lax.dot_general(a, b, (((1,), (1,)), ((), ())), preferred_element_type=F32)


def _inproj_a_body(x_ref, g_ref, w_ref, gq_ref, gk_ref, o_ref, xn_ref, *, chunk):
    j = pl.program_id(1)
    tm = x_ref.shape[0]
    tn = w_ref.shape[1]

    @pl.when(j == 0)
    def _():
        def norm_chunk(c, carry):
            r = pl.multiple_of(c * chunk, chunk)
            xn_ref[pl.ds(r, chunk), :] = _rms(x_ref[pl.ds(r, chunk), :], g_ref[...]).astype(BF16)
            return carry

        lax.fori_loop(0, tm // chunk, norm_chunk, 0)

    y = _dot(xn_ref[...], w_ref[...])
    nq = A_WIDTH // tn

    def head_norm(gain_ref, scale):
        for h in range(tn // A_HEAD_DIM):
            yh = y[:, h * A_HEAD_DIM:(h + 1) * A_HEAD_DIM]
            ms = jnp.mean(yh * yh, axis=-1, keepdims=True)
            o_ref[:, h * A_HEAD_DIM:(h + 1) * A_HEAD_DIM] = (
                yh * lax.rsqrt(ms + EPS) * (gain_ref[...] * scale)).astype(BF16)

    @pl.when(j < nq)
    def _():
        head_norm(gq_ref, A_HEAD_DIM ** -0.5)

    @pl.when((j >= nq) & (j < 2 * nq))
    def _():
        head_norm(gk_ref, 1.0)

    @pl.when(j >= 2 * nq)
    def _():
        o_ref[...] = y.astype(BF16)


def _inproj_a(x, g, w_a, gq, gk):
    n = x.shape[0]
    tm = min(TM_INPROJ, n)
    tn = TN_INPROJ
    ncol = w_a.shape[1]
    return pl.pallas_call(
        functools.partial(_inproj_a_body, chunk=128),
        grid=(n // tm, ncol // tn),
        in_specs=[
            pl.BlockSpec((tm, D_MODEL), lambda i, j: (i, 0)),
            pl.BlockSpec((1, D_MODEL), lambda i, j: (0, 0)),
            pl.BlockSpec((D_MODEL, tn), lambda i, j: (0, j)),
            pl.BlockSpec((1, A_HEAD_DIM), lambda i, j: (0, 0)),
            pl.BlockSpec((1, A_HEAD_DIM), lambda i, j: (0, 0)),
        ],
        out_specs=pl.BlockSpec((tm, tn), lambda i, j: (i, j)),
        out_shape=jax.ShapeDtypeStruct((n, ncol), BF16),
        scratch_shapes=[pltpu.VMEM((tm, D_MODEL), BF16)],
        compiler_params=_cparams(("parallel", "arbitrary")),
    )(x, g, w_a, gq, gk)


def _rope_padded(t, cos, sin):
    return t * cos + pltpu.roll(t, 2 * (QK_ROPE // 2), 1) * sin


def _inproj_lat_body(x_ref, g_ref, wq_ref, wkv_ref, wkr_ref, gq_ref, gkv_ref, gkr_ref, cos_ref, sin_ref,
                     cq_ref, ckv_ref, kr_ref):
    xn = _rms(x_ref[...], g_ref[...]).astype(BF16)
    cq_ref[...] = _rms(_dot(xn, wq_ref[...]), gq_ref[...]).astype(BF16)
    ckv_ref[...] = _rms(_dot(xn, wkv_ref[...]), gkv_ref[...]).astype(BF16)
    kr = _dot(xn, wkr_ref[...])
    ms = jnp.sum(kr * kr, axis=-1, keepdims=True) * (1.0 / QK_ROPE)
    krn = kr * lax.rsqrt(ms + EPS) * gkr_ref[...]
    kr_ref[...] = _rope_padded(krn, cos_ref[...], sin_ref[...]).astype(BF16)


def _inproj_lat(x, g, wq, wkv, wkr, gq, gkv, gkr, cos_pad, sin_pad, seq):
    n = x.shape[0]
    tm = min(TM_LAT, seq)
    nseq = seq // tm
    const = lambda i: (0, 0)
    return pl.pallas_call(
        _inproj_lat_body,
        grid=(n // tm,),
        in_specs=[
            pl.BlockSpec((tm, D_MODEL), lambda i: (i, 0)),
            pl.BlockSpec((1, D_MODEL), const),
            pl.BlockSpec((D_MODEL, Q_LORA), const),
            pl.BlockSpec((D_MODEL, KV_LORA), const),
            pl.BlockSpec((D_MODEL, LANES), const),
            pl.BlockSpec((1, Q_LORA), const),
            pl.BlockSpec((1, KV_LORA), const),
            pl.BlockSpec((1, LANES), const),
            pl.BlockSpec((tm, LANES), lambda i: (i % nseq, 0)),
            pl.BlockSpec((tm, LANES), lambda i: (i % nseq, 0)),
        ],
        out_specs=[
            pl.BlockSpec((tm, Q_LORA), lambda i: (i, 0)),
            pl.BlockSpec((tm, KV_LORA), lambda i: (i, 0)),
            pl.BlockSpec((tm, LANES), lambda i: (i, 0)),
        ],
        out_shape=[
            jax.ShapeDtypeStruct((n, Q_LORA), BF16),
            jax.ShapeDtypeStruct((n, KV_LORA), BF16),
            jax.ShapeDtypeStruct((n, LANES), BF16),
        ],
        compiler_params=_cparams(("parallel",)),
    )(x, g, wq, wkv, wkr, gq, gkv, gkr, cos_pad, sin_pad)


def _mla_up_body(cq_ref, ckv_ref, kr_ref, wq_ref, wkv_ref, gqn_ref, gqr_ref, gkn_ref, cos_ref, sin_ref,
                 q_ref, k_ref, v_ref):
    cq = cq_ref[...]
    ckv = ckv_ref[...]
    kr = kr_ref[...]
    cos = cos_ref[...]
    sin = sin_ref[...]
    scale = (QK_NOPE + QK_ROPE) ** -0.5 * math.log2(math.e)
    ones = jnp.ones((cq.shape[0], V_HEAD), BF16)
    for h in range(MLA_HEADS):
        lo = h * MLA_HEAD_PAD
        q = _dot(cq, wq_ref[:, lo:lo + MLA_HEAD_PAD])
        qn = _rms(q[:, :LANES], gqn_ref[...] * scale)
        qr = q[:, LANES:]
        ms = jnp.sum(qr * qr, axis=-1, keepdims=True) * (1.0 / QK_ROPE)
        qr = _rope_padded(qr * lax.rsqrt(ms + EPS) * (gqr_ref[...] * scale), cos, sin)
        q_ref[:, lo:lo + LANES] = qn.astype(BF16)
        q_ref[:, lo + LANES:lo + MLA_HEAD_PAD] = qr.astype(BF16)
        kv = _dot(ckv, wkv_ref[:, lo:lo + MLA_HEAD_PAD])
        k_ref[:, lo:lo + LANES] = _rms(kv[:, :LANES], gkn_ref[...]).astype(BF16)
        k_ref[:, lo + LANES:lo + MLA_HEAD_PAD] = kr
        v_ref[:, lo:lo + V_HEAD] = kv[:, LANES:].astype(BF16)
        v_ref[:, lo + V_HEAD:lo + 2 * V_HEAD] = ones


def _mla_up(cq, ckv, kr, wq_pad, wkv, gqn, gqr_pad, gkn, cos_pad, sin_pad, seq):
    n = cq.shape[0]
    tm = min(TM_MLAUP, seq)
    nseq = seq // tm
    const = lambda i: (0, 0)
    row = lambda i: (i, 0)
    wide = MLA_HEADS * MLA_HEAD_PAD
    return pl.pallas_call(
        _mla_up_body,
        grid=(n // tm,),
        in_specs=[
            pl.BlockSpec((tm, Q_LORA), row),
            pl.BlockSpec((tm, KV_LORA), row),
            pl.BlockSpec((tm, LANES), row),
            pl.BlockSpec((Q_LORA, wide), const),
            pl.BlockSpec((KV_LORA, wide), const),
            pl.BlockSpec((1, LANES), const),
            pl.BlockSpec((1, LANES), const),
            pl.BlockSpec((1, LANES), const),
            pl.BlockSpec((tm, LANES), lambda i: (i % nseq, 0)),
            pl.BlockSpec((tm, LANES), lambda i: (i % nseq, 0)),
        ],
        out_specs=[
            pl.BlockSpec((tm, wide), row),
            pl.BlockSpec((tm, wide), row),
            pl.BlockSpec((tm, wide), row),
        ],
        out_shape=[
            jax.ShapeDtypeStruct((n, wide), BF16),
            jax.ShapeDtypeStruct((n, wide), BF16),
            jax.ShapeDtypeStruct((n, wide), BF16),
        ],
        compiler_params=_cparams(("parallel",)),
    )(cq, ckv, kr, wq_pad, wkv, gqn, gqr_pad, gkn, cos_pad, sin_pad)


def _mla_flash_body(q_ref, k_ref, v_ref, o_ref, m_ref, acc_ref, s0_ref, s1_ref, *, tk):
    nk = k_ref.shape[0] // tk
    m_ref[...] = jnp.full(m_ref.shape, -jnp.inf, F32)
    acc_ref[...] = jnp.zeros(acc_ref.shape, F32)

    def scores(c, s_ref):
        r = pl.multiple_of(c * tk, tk)
        s_ref[...] = _dot_nt(q_ref[...], k_ref[pl.ds(r, tk), :])

    def update(c, s_ref):
        r = pl.multiple_of(c * tk, tk)
        s = s_ref[...]
        m_old = m_ref[...]
        m_new = jnp.maximum(m_old, jnp.max(s, axis=-1, keepdims=True))
        alpha = jnp.exp2(m_old - m_new)
        p = jnp.exp2((s - jnp.tile(m_new, (1, tk // LANES))).astype(BF16))
        acc_ref[...] = jnp.tile(alpha, (1, 2)) * acc_ref[...] + _dot(p, v_ref[pl.ds(r, tk), :])
        m_ref[...] = m_new

    scores(0, s0_ref)

    def pair(jj, carry):
        c = 2 * jj
        scores(c + 1, s1_ref)
        update(c, s0_ref)
        scores(c + 2, s0_ref)
        update(c + 1, s1_ref)
        return carry

    lax.fori_loop(0, nk // 2 - 1, pair, 0)
    scores(nk - 1, s1_ref)
    update(nk - 2, s0_ref)
    update(nk - 1, s1_ref)
    o_ref[...] = (acc_ref[:, :V_HEAD] / acc_ref[:, V_HEAD:]).astype(o_ref.dtype)


def _mla_flash(qm, km, vm):
    b, s, _ = qm.shape
    tq = min(TQ_MLA, s)
    tk = min(TK_MLA, s // 2)
    return pl.pallas_call(
        functools.partial(_mla_flash_body, tk=tk),
        grid=(b, MLA_HEADS, s // tq),
        in_specs=[
            pl.BlockSpec((None, tq, MLA_HEAD_PAD), lambda bi, h, i: (bi, i, h)),
            pl.BlockSpec((None, s, MLA_HEAD_PAD), lambda bi, h, i: (bi, 0, h)),
            pl.BlockSpec((None, s, 2 * V_HEAD), lambda bi, h, i: (bi, 0, h)),
        ],
        out_specs=pl.BlockSpec((None, tq, V_HEAD), lambda bi, h, i: (bi, i, h)),
        out_shape=jax.ShapeDtypeStruct((b, s, MLA_WIDTH), BF16),
        scratch_shapes=[pltpu.VMEM((tq, LANES), F32), pltpu.VMEM((tq, 2 * V_HEAD), F32),
                        pltpu.VMEM((tq, tk), F32), pltpu.VMEM((tq, tk), F32)],
        compiler_params=_cparams(("parallel", "parallel", "arbitrary")),
    )(qm, km, vm)


def _dilated_body(q_ref, k0_ref, kp_ref, kn_ref, v0_ref, vp_ref, vn_ref, o_ref, lse_ref, kw_ref, vw_ref,
                  *, dil, nq):
    i = pl.program_id(2)
    tq = q_ref.shape[0]
    win = tq + 2 * N_SIDE
    kw_ref[0:N_SIDE, :] = kp_ref[...]
    kw_ref[N_SIDE:N_SIDE + tq, :] = k0_ref[...]
    kw_ref[N_SIDE + tq:win, :] = kn_ref[...]
    vw_ref[0:N_SIDE, :] = vp_ref[...]
    vw_ref[N_SIDE:N_SIDE + tq, :] = v0_ref[...]
    vw_ref[N_SIDE + tq:win, :] = vn_ref[...]

    @pl.when(i == nq - 1)
    def _():
        kw_ref[N_SIDE + tq:win, :] = jnp.zeros((N_SIDE, A_WIDTH), BF16)
        vw_ref[N_SIDE + tq:win, :] = jnp.zeros((N_SIDE, A_WIDTH), BF16)

    row = lax.broadcasted_iota(jnp.int32, (tq, win), 0)
    col = lax.broadcasted_iota(jnp.int32, (tq, win), 1)
    rel = col - N_SIDE - row
    arel = jnp.abs(rel)
    lower_ok = (rel >= 0) | (row >= N_SIDE) | (i > 0)
    upper_ok = (rel < N_SIDE) | (row < tq - N_SIDE) | (i < nq - 1)
    valid = (arel <= N_SIDE) & lower_ok & upper_ok
    dist = jnp.where(valid, (dil * arel).astype(F32), jnp.inf)

    lane = lax.broadcasted_iota(jnp.int32, (tq, LANES), 1)
    lse_all = jnp.zeros((tq, LANES), F32)
    for h in range(A_HEADS):
        sl = slice(h * A_HEAD_DIM, (h + 1) * A_HEAD_DIM)
        slope = 2.0 ** (-8.0 * (h + 1) / A_HEADS)
        s = _dot_nt(q_ref[:, sl], kw_ref[:, sl]) - slope * dist
        m = jnp.max(s, axis=-1, keepdims=True)
        p = jnp.exp(s - m)
        l = jnp.sum(p, axis=-1, keepdims=True)
        o_ref[:, sl] = (_dot(p.astype(BF16), vw_ref[:, sl]) / l).astype(o_ref.dtype)
        lse_all = jnp.where(lane == h, m + jnp.log(l), lse_all)
    lse_ref[...] = lse_all


def _dilated_branch(qkv, dil):
    b, s, _ = qkv.shape
    sub = s // dil
    tq = TQ_DIL
    nq = sub // tq
    nhalo = sub // N_SIDE
    per = tq // N_SIDE
    view = qkv.reshape(b, sub, dil * 3 * A_WIDTH)

    def own(which):
        return pl.BlockSpec((None, tq, A_WIDTH), lambda bi, r, i: (bi, i, 3 * r + which))

    def prev(which):
        return pl.BlockSpec((None, N_SIDE, A_WIDTH),
                            lambda bi, r, i: (bi, jnp.maximum(per * i - 1, 0), 3 * r + which))

    def nxt(which):
        return pl.BlockSpec((None, N_SIDE, A_WIDTH),
                            lambda bi, r, i: (bi, jnp.minimum(per * (i + 1), nhalo - 1), 3 * r + which))

    o, lse = pl.pallas_call(
        functools.partial(_dilated_body, dil=dil, nq=nq),
        grid=(b, dil, nq),
        in_specs=[own(0), own(1), prev(1), nxt(1), own(2), prev(2), nxt(2)],
        out_specs=[
            pl.BlockSpec((None, tq, A_WIDTH), lambda bi, r, i: (bi, i, r)),
            pl.BlockSpec((None, tq, LANES), lambda bi, r, i: (bi, i, r)),
        ],
        out_shape=[
            jax.ShapeDtypeStruct((b, sub, dil * A_WIDTH), BF16),
            jax.ShapeDtypeStruct((b, sub, dil * LANES), F32),
        ],
        scratch_shapes=[
            pltpu.VMEM((tq + 2 * N_SIDE, A_WIDTH), BF16),
            pltpu.VMEM((tq + 2 * N_SIDE, A_WIDTH), BF16),
        ],
        compiler_params=_cparams(("parallel", "parallel", "arbitrary")),
    )(view, view, view, view, view, view, view)
    return o.reshape(b * s, A_WIDTH), lse.reshape(b * s, LANES)


def _outproj_body(o1_ref, o2_ref, o3_ref, l1_ref, l2_ref, l3_ref, ob_ref, ga_ref, gb_ref, w_ref, x_ref,
                  y_ref, mix_ref, oa_ref):
    j = pl.program_id(1)

    @pl.when(j == 0)
    def _():
        l1, l2, l3 = l1_ref[...], l2_ref[...], l3_ref[...]
        mx = jnp.maximum(jnp.maximum(l1, l2), l3)
        e1, e2, e3 = jnp.exp(l1 - mx), jnp.exp(l2 - mx), jnp.exp(l3 - mx)
        inv = 1.0 / (e1 + e2 + e3)
        w1, w2, w3 = e1 * inv, e2 * inv, e3 * inv
        ss = jnp.zeros((l1.shape[0], 1), F32)
        for h in range(A_HEADS):
            sl = slice(h * A_HEAD_DIM, (h + 1) * A_HEAD_DIM)
            oa = (w1[:, h:h + 1] * o1_ref[:, sl].astype(F32) + w2[:, h:h + 1] * o2_ref[:, sl].astype(F32)
                  + w3[:, h:h + 1] * o3_ref[:, sl].astype(F32))
            oa_ref[:, sl] = oa
            ss = ss + jnp.sum(oa * oa, axis=-1, keepdims=True)
        inv_a = lax.rsqrt(ss * (1.0 / A_WIDTH) + EPS)
        mix_ref[:, :A_WIDTH] = (oa_ref[...] * inv_a * ga_ref[...]).astype(BF16)
        mix_ref[:, A_WIDTH:] = _rms(ob_ref[...].astype(F32), gb_ref[...]).astype(BF16)

    y_ref[...] = x_ref[...] + _dot(mix_ref[...], w_ref[...])


def _outproj(o1, o2, o3, l1, l2, l3, ob, ga, gb, w_o, x):
    n = x.shape[0]
    tm = min(TM_OUT, n)
    tn = TN_OUT
    row = lambda i, j: (i, 0)
    const = lambda i, j: (0, 0)
    mixw = A_WIDTH + MLA_WIDTH
    return pl.pallas_call(
        _outproj_body,
        grid=(n // tm, D_MODEL // tn),
        in_specs=[
            pl.BlockSpec((tm, A_WIDTH), row), pl.BlockSpec((tm, A_WIDTH), row), pl.BlockSpec((tm, A_WIDTH), row),
            pl.BlockSpec((tm, LANES), row), pl.BlockSpec((tm, LANES), row), pl.BlockSpec((tm, LANES), row),
            pl.BlockSpec((tm, MLA_WIDTH), row),
            pl.BlockSpec((1, A_WIDTH), const), pl.BlockSpec((1, MLA_WIDTH), const),
            pl.BlockSpec((mixw, tn), lambda i, j: (0, j)),
            pl.BlockSpec((tm, tn), lambda i, j: (i, j)),
        ],
        out_specs=pl.BlockSpec((tm, tn), lambda i, j: (i, j)),
        out_shape=jax.ShapeDtypeStruct((n, D_MODEL), F32),
        scratch_shapes=[pltpu.VMEM((tm, mixw), BF16), pltpu.VMEM((tm, A_WIDTH), F32)],
        compiler_params=_cparams(("parallel", "arbitrary")),
    )(o1, o2, o3, l1, l2, l3, ob, ga, gb, w_o, x)


def _router_body(x_ref, g_ref, wr_ref, h_ref, aff_ref):
    h = _rms(x_ref[...], g_ref[...])
    h_ref[...] = h
    logits = lax.dot_general(wr_ref[...], h, (((1,), (1,)), ((), ())), preferred_element_type=F32,
                             precision=lax.Precision.HIGHEST)
    m = jnp.max(logits, axis=0, keepdims=True)
    e = jnp.exp(logits - m)
    aff_ref[...] = e / jnp.sum(e, axis=0, keepdims=True)


def _router(x2, g, wr_t):
    n = x2.shape[0]
    tm = min(TM_ROUTER, n)
    return pl.pallas_call(
        _router_body,
        grid=(n // tm,),
        in_specs=[
            pl.BlockSpec((tm, D_MODEL), lambda i: (i, 0)),
            pl.BlockSpec((1, D_MODEL), lambda i: (0, 0)),
            pl.BlockSpec((N_EXPERTS, D_MODEL), lambda i: (0, 0)),
        ],
        out_specs=[
            pl.BlockSpec((tm, D_MODEL), lambda i: (i, 0)),
            pl.BlockSpec((N_EXPERTS, tm), lambda i: (0, i)),
        ],
        out_shape=[
            jax.ShapeDtypeStruct((n, D_MODEL), F32),
            jax.ShapeDtypeStruct((N_EXPERTS, n), F32),
        ],
        compiler_params=_cparams(("parallel",)),
    )(x2, g, wr_t)


def _sum_all(x):
    return jnp.sum(jnp.sum(x, axis=1, keepdims=True), axis=0, keepdims=True)


def _topk_body(a_ref, idx_ref, gate_ref, *, cap):
    a = a_ref[...]
    nc = a.shape[0]
    bits = pltpu.bitcast(a, jnp.int32)
    thr = jnp.zeros((1, 1), jnp.int32)
    for bit in range(30, -1, -1):
        cand = thr + (1 << bit)
        cnt = _sum_all((bits >= cand).astype(F32))
        thr = jnp.where(cnt >= cap, cand, thr)
    gt = bits > thr
    eq = bits == thr
    eqf = eq.astype(F32)
    need = cap - _sum_all(gt.astype(F32))

    li = lax.broadcasted_iota(jnp.int32, (LANES, LANES), 0)
    lj = lax.broadcasted_iota(jnp.int32, (LANES, LANES), 1)
    upper_incl = (li <= lj).astype(BF16)
    ci = lax.broadcasted_iota(jnp.int32, (nc, nc), 0)
    cj = lax.broadcasted_iota(jnp.int32, (nc, nc), 1)
    lower_strict = (cj < ci).astype(BF16)
    upper_incl_c = (ci <= cj).astype(BF16)

    p_eq = _dot(eqf.astype(BF16), upper_incl)
    tot_eq = jnp.broadcast_to(p_eq[:, LANES - 1:LANES], (nc, LANES))
    off_eq = _dot(lower_strict, tot_eq.astype(BF16))
    sel = gt | (eq & ((p_eq - eqf + off_eq) < need))
    selb = sel.astype(F32).astype(BF16)

    p_sel = _dot(selb, upper_incl)
    tot_row = _dot_nt(jnp.ones((8, LANES), BF16), selb)
    offi_row = _dot(tot_row.astype(BF16), upper_incl_c)[0:1, :]
    offe_row = offi_row - tot_row[0:1, :]

    slot_c = lax.broadcasted_iota(jnp.int32, (cap, nc), 0).astype(F32)
    chunk_of = jnp.sum((offi_row <= slot_c).astype(F32), axis=1, keepdims=True)
    onehot = lax.broadcasted_iota(jnp.int32, (cap, nc), 1).astype(F32) == chunk_of
    onehot_b = onehot.astype(F32).astype(BF16)
    run = _dot(onehot_b, p_sel.astype(BF16))
    offe = jnp.sum(jnp.where(onehot, offe_row, 0.0), axis=1, keepdims=True)
    slot_l = lax.broadcasted_iota(jnp.int32, (cap, LANES), 0).astype(F32)
    lane_of = jnp.sum(((run + offe) <= slot_l).astype(F32), axis=1, keepdims=True)
    idx_ref[...] = (chunk_of * LANES + lane_of).astype(jnp.int32)

    a1 = a.astype(BF16)
    r1 = a - a1.astype(F32)
    a2 = r1.astype(BF16)
    a3 = (r1 - a2.astype(F32)).astype(BF16)
    rows = _dot(onehot_b, a1) + _dot(onehot_b, a2) + _dot(onehot_b, a3)
    lanes = lax.broadcasted_iota(jnp.int32, (cap, LANES), 1).astype(F32)
    gate_ref[...] = jnp.sum(jnp.where(lanes == lane_of, rows, 0.0), axis=1, keepdims=True)


def _topk(aff_t, cap):
    e, n = aff_t.shape
    nc = n // LANES
    return pl.pallas_call(
        functools.partial(_topk_body, cap=cap),
        grid=(e,),
        in_specs=[pl.BlockSpec((None, nc, LANES), lambda i: (i, 0, 0))],
        out_specs=[
            pl.BlockSpec((None, cap, 1), lambda i: (i, 0, 0)),
            pl.BlockSpec((None, cap, 1), lambda i: (i, 0, 0)),
        ],
        out_shape=[
            jax.ShapeDtypeStruct((e, cap, 1), jnp.int32),
            jax.ShapeDtypeStruct((e, cap, 1), F32),
        ],
        compiler_params=_cparams(("parallel",)),
    )(aff_t.reshape(e, nc, LANES))


def _row_copy(src_hbm, tok, dst_ref, r, sem):
    return pltpu.make_async_copy(src_hbm.at[pl.ds(tok, 1), :], dst_ref.at[pl.ds(r, 1), :], sem)


def _moe_up_body(idx_ref, h_hbm, gate_ref, wg_ref, wu_ref, hid_ref, xe_ref, stage_ref, sem, *, cap, rows):
    e = pl.program_id(0)
    f = pl.program_id(1)

    @pl.when(f == 0)
    def _():
        nchunk = cap // rows

        def start(c, slot):
            def body(r, carry):
                tok = idx_ref[e * cap + c * rows + r]
                _row_copy(h_hbm, tok, stage_ref.at[slot], r, sem.at[slot]).start()
                return carry
            lax.fori_loop(0, rows, body, 0, unroll=DMA_UNROLL)

        def wait(slot):
            pltpu.make_async_copy(h_hbm.at[pl.ds(0, rows), :], stage_ref.at[slot], sem.at[slot]).wait()

        start(0, 0)
        for c in range(nchunk):
            slot = c % 2
            if c + 1 < nchunk:
                start(c + 1, 1 - slot)
            wait(slot)
            xe_ref[c * rows:(c + 1) * rows, :] = stage_ref[slot].astype(BF16)

    xe = xe_ref[...]
    g = _dot(xe, wg_ref[...].astype(BF16))
    u = _dot(xe, wu_ref[...].astype(BF16))
    hid = g / (1.0 + jnp.exp(-g)) * u * gate_ref[...]
    hid_ref[...] = hid.astype(BF16)


def _moe_up(idx_flat, h2, gate, w_gate, w_up, cap):
    n = h2.shape[0]
    tf = TF_MOE
    rows = min(G_GATHER, cap)
    grid_spec = pltpu.PrefetchScalarGridSpec(
        num_scalar_prefetch=1,
        grid=(N_EXPERTS, EXPERT_FF // tf),
        in_specs=[
            pl.BlockSpec(memory_space=pl.ANY),
            pl.BlockSpec((None, cap, 1), lambda e, f, idx: (e, 0, 0)),
            pl.BlockSpec((None, D_MODEL, tf), lambda e, f, idx: (e, 0, f)),
            pl.BlockSpec((None, D_MODEL, tf), lambda e, f, idx: (e, 0, f)),
        ],
        out_specs=pl.BlockSpec((None, cap, tf), lambda e, f, idx: (e, 0, f)),
        scratch_shapes=[
            pltpu.VMEM((cap, D_MODEL), BF16),
            pltpu.VMEM((2, rows, D_MODEL), F32),
            pltpu.SemaphoreType.DMA((2,)),
        ],
    )
    return pl.pallas_call(
        functools.partial(_moe_up_body, cap=cap, rows=rows),
        grid_spec=grid_spec,
        out_shape=jax.ShapeDtypeStruct((N_EXPERTS, cap, EXPERT_FF), BF16),
        compiler_params=_cparams(("arbitrary", "arbitrary")),
    )(idx_flat, h2, gate, w_gate, w_up)


def _moe_down_body(idx_ref, hid_ref, wd_ref, x_hbm, out_hbm, acc_ref, res_ref, sem, *, cap, rows):
    del x_hbm
    e = pl.program_id(0)
    rb = pl.program_id(1)
    kt = pl.program_id(2)
    last_kt = pl.num_programs(2) - 1
    base = e * cap + rb * rows
    gather_sem, scatter_sem = sem.at[0], sem.at[1]

    def wait_scatter():
        pltpu.make_async_copy(res_ref, out_hbm.at[pl.ds(0, rows), :], scatter_sem).wait()

    @pl.when(kt == 0)
    def _():
        @pl.when((e > 0) | (rb > 0))
        def _():
            wait_scatter()

        def start(r, carry):
            pltpu.make_async_copy(out_hbm.at[pl.ds(idx_ref[base + r], 1), :], res_ref.at[pl.ds(r, 1), :],
                                  gather_sem).start()
            return carry
        lax.fori_loop(0, rows, start, 0, unroll=DMA_UNROLL)
        acc_ref[...] = jnp.zeros(acc_ref.shape, F32)

    acc_ref[...] += _dot(hid_ref[...], wd_ref[...].astype(BF16))

    @pl.when(kt == last_kt)
    def _():
        pltpu.make_async_copy(out_hbm.at[pl.ds(0, rows), :], res_ref, gather_sem).wait()
        res_ref[...] += acc_ref[...]

        def start(r, carry):
            pltpu.make_async_copy(res_ref.at[pl.ds(r, 1), :], out_hbm.at[pl.ds(idx_ref[base + r], 1), :],
                                  scatter_sem).start()
            return carry
        lax.fori_loop(0, rows, start, 0, unroll=DMA_UNROLL)

        @pl.when((e == pl.num_programs(0) - 1) & (rb == pl.num_programs(1) - 1))
        def _():
            wait_scatter()


def _moe_down(idx_flat, hid, w_down, x2, cap):
    n = x2.shape[0]
    rows = min(R_DOWN, cap)
    tk = TK_DOWN
    grid_spec = pltpu.PrefetchScalarGridSpec(
        num_scalar_prefetch=1,
        grid=(N_EXPERTS, cap // rows, EXPERT_FF // tk),
        in_specs=[
            pl.BlockSpec((None, rows, tk), lambda e, rb, kt, idx: (e, rb, kt)),
            pl.BlockSpec((None, tk, D_MODEL), lambda e, rb, kt, idx: (e, kt, 0)),
            pl.BlockSpec(memory_space=pl.ANY),
        ],
        out_specs=pl.BlockSpec(memory_space=pl.ANY),
        scratch_shapes=[
            pltpu.VMEM((rows, D_MODEL), F32),
            pltpu.VMEM((rows, D_MODEL), F32),
            pltpu.SemaphoreType.DMA((2,)),
        ],
    )
    return pl.pallas_call(
        functools.partial(_moe_down_body, cap=cap, rows=rows),
        grid_spec=grid_spec,
        out_shape=jax.ShapeDtypeStruct((n, D_MODEL), F32),
        input_output_aliases={3: 0},
        compiler_params=_cparams(("arbitrary", "arbitrary", "arbitrary")),
    )(idx_flat, hid, w_down, x2)


def _pad_rope_cols(t):
    half = QK_ROPE // 2
    z = jnp.zeros(t.shape[:-1] + (half,), t.dtype)
    return jnp.concatenate([t[..., :half], z, t[..., half:], z], axis=-1)


def _rope_tables_padded(seq):
    inv = ROPE_THETA ** (-jnp.arange(0, QK_ROPE, 2, dtype=F32) / QK_ROPE)
    ang = jnp.arange(seq, dtype=F32)[:, None] * inv[None, :]
    cos, sin = jnp.cos(ang), jnp.sin(ang)
    z = jnp.zeros_like(cos)
    return jnp.concatenate([cos, z, cos, z], axis=-1), jnp.concatenate([-sin, z, sin, z], axis=-1)


def _prepare_weights(w_in, mla_w_q_b, mla_w_kv_b, w_o, w_router, mla_q_rope_norm_g, mla_k_rope_norm_g):
    qkv_cols = 3 * A_WIDTH
    w_a = w_in[:, :qkv_cols].astype(BF16)
    w_ql = w_in[:, qkv_cols:qkv_cols + Q_LORA].astype(BF16)
    w_kvl = w_in[:, qkv_cols + Q_LORA:qkv_cols + Q_LORA + KV_LORA].astype(BF16)
    w_kr = _pad_rope_cols(w_in[:, qkv_cols + Q_LORA + KV_LORA:]).astype(BF16)
    wq = mla_w_q_b.reshape(Q_LORA, MLA_HEADS, QK_NOPE + QK_ROPE)
    wq_pad = jnp.concatenate([wq[..., :QK_NOPE], _pad_rope_cols(wq[..., QK_NOPE:])], axis=-1)
    wq_pad = wq_pad.reshape(Q_LORA, MLA_HEADS * MLA_HEAD_PAD).astype(BF16)
    return dict(
        w_a=w_a, w_ql=w_ql, w_kvl=w_kvl, w_kr=w_kr, wq_pad=wq_pad,
        wkv=mla_w_kv_b.astype(BF16), w_o=w_o.astype(BF16), wr_t=w_router.T,
        gqr_pad=_pad_rope_cols(mla_q_rope_norm_g)[None, :], gkr_pad=_pad_rope_cols(mla_k_rope_norm_g)[None, :],
    )


def _encoder_layer(x3, pw, norm_mix_g, a_q_norm_g, a_k_norm_g, mla_q_a_norm_g, mla_kv_a_norm_g,
                   mla_q_nope_norm_g, mla_k_nope_norm_g, out_norm_a_g, out_norm_b_g, norm_ffn_g,
                   w_gate, w_up, w_down):
    b, s, d = x3.shape
    n = b * s
    x = x3.reshape(n, d)
    row = lambda v: v[None, :]
    cos_pad, sin_pad = _rope_tables_padded(s)

    qkv = _inproj_a(x, row(norm_mix_g), pw["w_a"], row(a_q_norm_g), row(a_k_norm_g))
    qkv3 = qkv.reshape(b, s, 3 * A_WIDTH)
    branches = [_dilated_branch(qkv3, dil) for _, dil in DILATED_PATTERNS]

    cq, ckv, kr = _inproj_lat(x, row(norm_mix_g), pw["w_ql"], pw["w_kvl"], pw["w_kr"], row(mla_q_a_norm_g),
                              row(mla_kv_a_norm_g), pw["gkr_pad"], cos_pad, sin_pad, s)
    qm, km, vm = _mla_up(cq, ckv, kr, pw["wq_pad"], pw["wkv"], row(mla_q_nope_norm_g), pw["gqr_pad"],
                         row(mla_k_nope_norm_g), cos_pad, sin_pad, s)
    wide = MLA_HEADS * MLA_HEAD_PAD
    ob = _mla_flash(qm.reshape(b, s, wide), km.reshape(b, s, wide), vm.reshape(b, s, wide))

    (o1, l1), (o2, l2), (o3, l3) = branches
    x2 = _outproj(o1, o2, o3, l1, l2, l3, ob.reshape(n, MLA_WIDTH), row(out_norm_a_g), row(out_norm_b_g),
                  pw["w_o"], x)

    h2, aff_t = _router(x2, row(norm_ffn_g), pw["wr_t"])
    cap = CAPACITY_FACTOR * n // N_EXPERTS
    idx, gate = _topk(aff_t, cap)
    idx_flat = idx.reshape(N_EXPERTS * cap)
    hid = _moe_up(idx_flat, h2, gate, w_gate, w_up, cap)
    y = _moe_down(idx_flat, hid, w_down, x2, cap)
    return y.reshape(b, s, d)


def kernel(x_prompt, x_sample, norm_mix_g, w_in, a_q_norm_g, a_k_norm_g, mla_q_a_norm_g, mla_w_q_b, mla_kv_a_norm_g, mla_w_kv_b, mla_q_nope_norm_g, mla_q_rope_norm_g, mla_k_nope_norm_g, mla_k_rope_norm_g, out_norm_a_g, out_norm_b_g, w_o, norm_ffn_g, w_router, w_gate, w_up, w_down):
    depth = w_in.shape[0]
    prepared = [_prepare_weights(w_in[l], mla_w_q_b[l], mla_w_kv_b[l], w_o[l], w_router[l],
                                 mla_q_rope_norm_g[l], mla_k_rope_norm_g[l]) for l in range(depth)]

    def run_trunk(x):
        for l in range(depth):
            pw = prepared[l]
            x = _encoder_layer(x, pw, norm_mix_g[l], a_q_norm_g[l], a_k_norm_g[l], mla_q_a_norm_g[l],
                               mla_kv_a_norm_g[l], mla_q_nope_norm_g[l], mla_k_nope_norm_g[l],
                               out_norm_a_g[l], out_norm_b_g[l], norm_ffn_g[l], w_gate[l], w_up[l], w_down[l])
        return x

    return run_trunk(x_prompt), run_trunk(x_sample)
```

```python
import functools
import math

import jax
import jax.numpy as jnp
from jax import lax
from jax.experimental import pallas as pl
from jax.experimental.pallas import tpu as pltpu

F32 = jnp.float32
BF16 = jnp.bfloat16

D_MODEL = 4096
A_HEADS = 16
A_HEAD_DIM = 128
A_WIDTH = A_HEADS * A_HEAD_DIM
DILATED_PATTERNS = ((128, 1), (512, 4), (2048, 16))
DILATIONS = tuple(d for _, d in DILATED_PATTERNS)
MLA_HEADS = 16
QK_NOPE = 128
QK_ROPE = 64
V_HEAD = 128
Q_LORA = 1024
KV_LORA = 512
MLA_WIDTH = MLA_HEADS * V_HEAD
ROPE_THETA = 10000.0
N_EXPERTS = 16
EXPERT_FF = 2048
CAPACITY_FACTOR = 2
EPS = 1e-6

LANES = 128
MLA_HEAD_PAD = 2 * LANES
N_SIDE = 64

TM_INPROJ = 512
TN_INPROJ = 512
TM_LAT = 256
TM_MLAUP = 256
TQ_MLA = 512
TK_MLA = 2048
TQ_DIL = 128
TM_OUT = 512
TN_OUT = 512
TM_ROUTER = 256
TF_MOE = 256
TK_DOWN = 256
R_DOWN = 1024
G_GATHER = 256
DMA_UNROLL = 8
VMEM_LIMIT = 56 * 1024 * 1024


def _cparams(sem):
    return pltpu.CompilerParams(dimension_semantics=sem, vmem_limit_bytes=VMEM_LIMIT)


def _rms(x, g):
    ms = jnp.mean(x * x, axis=-1, keepdims=True)
    return x * lax.rsqrt(ms + EPS) * g


def _dot(a, b):
    return jnp.dot(a, b, preferred_element_type=F32)


def _dot_nt(a, b):
    return lax.dot_general(a, b, (((1,), (1,)), ((), ())), preferred_element_type=F32)


def _inproj_a_body(x_ref, g_ref, w_ref, gq_ref, gk_ref, *refs, chunk):
    nd = len(DILATIONS)
    out_refs, xn_ref, z_ref = refs[:nd], refs[nd], refs[nd + 1]
    j = pl.program_id(1)
    tm = x_ref.shape[0]
    tn = w_ref.shape[1]

    @pl.when(j == 0)
    def _():
        def norm_chunk(c, carry):
            r = pl.multiple_of(c * chunk, chunk)
            xn_ref[pl.ds(r, chunk), :] = _rms(x_ref[pl.ds(r, chunk), :], g_ref[...]).astype(BF16)
            return carry

        lax.fori_loop(0, tm // chunk, norm_chunk, 0)

    y = _dot(xn_ref[...], w_ref[...])
    nq = A_WIDTH // tn
    planes = tn // LANES

    def emit(plane):
        for c in range(planes):
            z_ref[c] = plane(y[:, c * LANES:(c + 1) * LANES])
        for o_ref, d in zip(out_refs, DILATIONS):
            for c in range(planes):
                cols = slice(c * LANES, (c + 1) * LANES)
                if d == 1:
                    o_ref[:, cols] = z_ref[c].astype(BF16)
                else:
                    for r in range(d):
                        o_ref[r, :, cols] = z_ref.at[c][pl.ds(r, tm // d, stride=d), :].astype(BF16)

    def head_norm(gain_ref, scale):
        def plane(yh):
            ms = jnp.mean(yh * yh, axis=-1, keepdims=True)
            return yh * lax.rsqrt(ms + EPS) * (gain_ref[...] * scale)
        return plane

    @pl.when(j < nq)
    def _():
        emit(head_norm(gq_ref, A_HEAD_DIM ** -0.5))

    @pl.when((j >= nq) & (j < 2 * nq))
    def _():
        emit(head_norm(gk_ref, 1.0))

    @pl.when(j >= 2 * nq)
    def _():
        emit(lambda yh: yh)


def _inproj_a(x, g, w_a, gq, gk, batch, seq):
    n = x.shape[0]
    tm = min(TM_INPROJ, seq)
    tn = TN_INPROJ
    ncol = w_a.shape[1]
    nseq = seq // tm
    out_specs, out_shape = [], []
    for d in DILATIONS:
        if d == 1:
            out_specs.append(pl.BlockSpec((tm, tn), lambda i, j: (i, j)))
            out_shape.append(jax.ShapeDtypeStruct((n, ncol), BF16))
        else:
            out_specs.append(pl.BlockSpec((None, d, tm // d, tn), lambda i, j: (i // nseq, 0, i % nseq, j)))
            out_shape.append(jax.ShapeDtypeStruct((batch, d, seq // d, ncol), BF16))
    outs = pl.pallas_call(
        functools.partial(_inproj_a_body, chunk=128),
        grid=(n // tm, ncol // tn),
        in_specs=[
            pl.BlockSpec((tm, D_MODEL), lambda i, j: (i, 0)),
            pl.BlockSpec((1, D_MODEL), lambda i, j: (0, 0)),
            pl.BlockSpec((D_MODEL, tn), lambda i, j: (0, j)),
            pl.BlockSpec((1, A_HEAD_DIM), lambda i, j: (0, 0)),
            pl.BlockSpec((1, A_HEAD_DIM), lambda i, j: (0, 0)),
        ],
        out_specs=out_specs,
        out_shape=out_shape,
        scratch_shapes=[pltpu.VMEM((tm, D_MODEL), BF16), pltpu.VMEM((tn // LANES, tm, LANES), F32)],
        compiler_params=_cparams(("parallel", "arbitrary")),
    )(x, g, w_a, gq, gk)
    return [o.reshape(batch, d, seq // d, ncol) for o, d in zip(outs, DILATIONS)]


def _rope_padded(t, cos, sin):
    return t * cos + pltpu.roll(t, 2 * (QK_ROPE // 2), 1) * sin


def _inproj_lat_body(x_ref, g_ref, wq_ref, wkv_ref, wkr_ref, gq_ref, gkv_ref, gkr_ref, cos_ref, sin_ref,
                     cq_ref, ckv_ref, kr_ref):
    xn = _rms(x_ref[...], g_ref[...]).astype(BF16)
    cq_ref[...] = _rms(_dot(xn, wq_ref[...]), gq_ref[...]).astype(BF16)
    ckv_ref[...] = _rms(_dot(xn, wkv_ref[...]), gkv_ref[...]).astype(BF16)
    kr = _dot(xn, wkr_ref[...])
    ms = jnp.sum(kr * kr, axis=-1, keepdims=True) * (1.0 / QK_ROPE)
    krn = kr * lax.rsqrt(ms + EPS) * gkr_ref[...]
    kr_ref[...] = _rope_padded(krn, cos_ref[...], sin_ref[...]).astype(BF16)


def _inproj_lat(x, g, wq, wkv, wkr, gq, gkv, gkr, cos_pad, sin_pad, seq):
    n = x.shape[0]
    tm = min(TM_LAT, seq)
    nseq = seq // tm
    const = lambda i: (0, 0)
    return pl.pallas_call(
        _inproj_lat_body,
        grid=(n // tm,),
        in_specs=[
            pl.BlockSpec((tm, D_MODEL), lambda i: (i, 0)),
            pl.BlockSpec((1, D_MODEL), const),
            pl.BlockSpec((D_MODEL, Q_LORA), const),
            pl.BlockSpec((D_MODEL, KV_LORA), const),
            pl.BlockSpec((D_MODEL, LANES), const),
            pl.BlockSpec((1, Q_LORA), const),
            pl.BlockSpec((1, KV_LORA), const),
            pl.BlockSpec((1, LANES), const),
            pl.BlockSpec((tm, LANES), lambda i: (i % nseq, 0)),
            pl.BlockSpec((tm, LANES), lambda i: (i % nseq, 0)),
        ],
        out_specs=[
            pl.BlockSpec((tm, Q_LORA), lambda i: (i, 0)),
            pl.BlockSpec((tm, KV_LORA), lambda i: (i, 0)),
            pl.BlockSpec((tm, LANES), lambda i: (i, 0)),
        ],
        out_shape=[
            jax.ShapeDtypeStruct((n, Q_LORA), BF16),
            jax.ShapeDtypeStruct((n, KV_LORA), BF16),
            jax.ShapeDtypeStruct((n, LANES), BF16),
        ],
        compiler_params=_cparams(("parallel",)),
    )(x, g, wq, wkv, wkr, gq, gkv, gkr, cos_pad, sin_pad)


def _mla_up_body(cq_ref, ckv_ref, kr_ref, wq_ref, wkv_ref, gqn_ref, gqr_ref, gkn_ref, cos_ref, sin_ref,
                 q_ref, k_ref, v_ref):
    cq = cq_ref[...]
    ckv = ckv_ref[...]
    kr = kr_ref[...]
    cos = cos_ref[...]
    sin = sin_ref[...]
    scale = (QK_NOPE + QK_ROPE) ** -0.5 * math.log2(math.e)
    ones = jnp.ones((cq.shape[0], V_HEAD), BF16)
    for h in range(MLA_HEADS):
        lo = h * MLA_HEAD_PAD
        q = _dot(cq, wq_ref[:, lo:lo + MLA_HEAD_PAD])
        qn = _rms(q[:, :LANES], gqn_ref[...] * scale)
        qr = q[:, LANES:]
        ms = jnp.sum(qr * qr, axis=-1, keepdims=True) * (1.0 / QK_ROPE)
        qr = _rope_padded(qr * lax.rsqrt(ms + EPS) * (gqr_ref[...] * scale), cos, sin)
        q_ref[:, lo:lo + LANES] = qn.astype(BF16)
        q_ref[:, lo + LANES:lo + MLA_HEAD_PAD] = qr.astype(BF16)
        kv = _dot(ckv, wkv_ref[:, lo:lo + MLA_HEAD_PAD])
        k_ref[:, lo:lo + LANES] = _rms(kv[:, :LANES], gkn_ref[...]).astype(BF16)
        k_ref[:, lo + LANES:lo + MLA_HEAD_PAD] = kr
        v_ref[:, lo:lo + V_HEAD] = kv[:, LANES:].astype(BF16)
        v_ref[:, lo + V_HEAD:lo + 2 * V_HEAD] = ones


def _mla_up(cq, ckv, kr, wq_pad, wkv, gqn, gqr_pad, gkn, cos_pad, sin_pad, seq):
    n = cq.shape[0]
    tm = min(TM_MLAUP, seq)
    nseq = seq // tm
    const = lambda i: (0, 0)
    row = lambda i: (i, 0)
    wide = MLA_HEADS * MLA_HEAD_PAD
    return pl.pallas_call(
        _mla_up_body,
        grid=(n // tm,),
        in_specs=[
            pl.BlockSpec((tm, Q_LORA), row),
            pl.BlockSpec((tm, KV_LORA), row),
            pl.BlockSpec((tm, LANES), row),
            pl.BlockSpec((Q_LORA, wide), const),
            pl.BlockSpec((KV_LORA, wide), const),
            pl.BlockSpec((1, LANES), const),
            pl.BlockSpec((1, LANES), const),
            pl.BlockSpec((1, LANES), const),
            pl.BlockSpec((tm, LANES), lambda i: (i % nseq, 0)),
            pl.BlockSpec((tm, LANES), lambda i: (i % nseq, 0)),
        ],
        out_specs=[
            pl.BlockSpec((tm, wide), row),
            pl.BlockSpec((tm, wide), row),
            pl.BlockSpec((tm, wide), row),
        ],
        out_shape=[
            jax.ShapeDtypeStruct((n, wide), BF16),
            jax.ShapeDtypeStruct((n, wide), BF16),
            jax.ShapeDtypeStruct((n, wide), BF16),
        ],
        compiler_params=_cparams(("parallel",)),
    )(cq, ckv, kr, wq_pad, wkv, gqn, gqr_pad, gkn, cos_pad, sin_pad)


def _mla_flash_body(q_ref, k_ref, v_ref, o_ref, m_ref, acc_ref, s0_ref, s1_ref, *, tk):
    nk = k_ref.shape[0] // tk
    m_ref[...] = jnp.full(m_ref.shape, -jnp.inf, F32)
    acc_ref[...] = jnp.zeros(acc_ref.shape, F32)

    def scores(c, s_ref):
        r = pl.multiple_of(c * tk, tk)
        s_ref[...] = _dot_nt(q_ref[...], k_ref[pl.ds(r, tk), :])

    def update(c, s_ref):
        r = pl.multiple_of(c * tk, tk)
        s = s_ref[...]
        m_old = m_ref[...]
        m_new = jnp.maximum(m_old, jnp.max(s, axis=-1, keepdims=True))
        alpha = jnp.exp2(m_old - m_new)
        p = jnp.exp2((s - jnp.tile(m_new, (1, tk // LANES))).astype(BF16))
        acc_ref[...] = jnp.tile(alpha, (1, 2)) * acc_ref[...] + _dot(p, v_ref[pl.ds(r, tk), :])
        m_ref[...] = m_new

    scores(0, s0_ref)

    def pair(jj, carry):
        c = 2 * jj
        scores(c + 1, s1_ref)
        update(c, s0_ref)
        scores(c + 2, s0_ref)
        update(c + 1, s1_ref)
        return carry

    lax.fori_loop(0, nk // 2 - 1, pair, 0)
    scores(nk - 1, s1_ref)
    update(nk - 2, s0_ref)
    update(nk - 1, s1_ref)
    o_ref[...] = (acc_ref[:, :V_HEAD] / acc_ref[:, V_HEAD:]).astype(o_ref.dtype)


def _mla_flash(qm, km, vm):
    b, s, _ = qm.shape
    tq = min(TQ_MLA, s)
    tk = min(TK_MLA, s // 2)
    return pl.pallas_call(
        functools.partial(_mla_flash_body, tk=tk),
        grid=(b, MLA_HEADS, s // tq),
        in_specs=[
            pl.BlockSpec((None, tq, MLA_HEAD_PAD), lambda bi, h, i: (bi, i, h)),
            pl.BlockSpec((None, s, MLA_HEAD_PAD), lambda bi, h, i: (bi, 0, h)),
            pl.BlockSpec((None, s, 2 * V_HEAD), lambda bi, h, i: (bi, 0, h)),
        ],
        out_specs=pl.BlockSpec((None, tq, V_HEAD), lambda bi, h, i: (bi, i, h)),
        out_shape=jax.ShapeDtypeStruct((b, s, MLA_WIDTH), BF16),
        scratch_shapes=[pltpu.VMEM((tq, LANES), F32), pltpu.VMEM((tq, 2 * V_HEAD), F32),
                        pltpu.VMEM((tq, tk), F32), pltpu.VMEM((tq, tk), F32)],
        compiler_params=_cparams(("parallel", "parallel", "arbitrary")),
    )(qm, km, vm)


def _dilated_body(q_ref, k0_ref, kp_ref, kn_ref, v0_ref, vp_ref, vn_ref, o_ref, lse_ref, kw_ref, vw_ref,
                  *, dil, nq):
    i = pl.program_id(2)
    tq = q_ref.shape[0]
    win = tq + 2 * N_SIDE
    kw_ref[0:N_SIDE, :] = kp_ref[...]
    kw_ref[N_SIDE:N_SIDE + tq, :] = k0_ref[...]
    kw_ref[N_SIDE + tq:win, :] = kn_ref[...]
    vw_ref[0:N_SIDE, :] = vp_ref[...]
    vw_ref[N_SIDE:N_SIDE + tq, :] = v0_ref[...]
    vw_ref[N_SIDE + tq:win, :] = vn_ref[...]

    @pl.when(i == nq - 1)
    def _():
        kw_ref[N_SIDE + tq:win, :] = jnp.zeros((N_SIDE, A_WIDTH), BF16)
        vw_ref[N_SIDE + tq:win, :] = jnp.zeros((N_SIDE, A_WIDTH), BF16)

    row = lax.broadcasted_iota(jnp.int32, (tq, win), 0)
    col = lax.broadcasted_iota(jnp.int32, (tq, win), 1)
    rel = col - N_SIDE - row
    arel = jnp.abs(rel)
    lower_ok = (rel >= 0) | (row >= N_SIDE) | (i > 0)
    upper_ok = (rel < N_SIDE) | (row < tq - N_SIDE) | (i < nq - 1)
    valid = (arel <= N_SIDE) & lower_ok & upper_ok
    dist = jnp.where(valid, (dil * arel).astype(F32), jnp.inf)

    lane = lax.broadcasted_iota(jnp.int32, (tq, LANES), 1)
    lse_all = jnp.zeros((tq, LANES), F32)
    for h in range(A_HEADS):
        sl = slice(h * A_HEAD_DIM, (h + 1) * A_HEAD_DIM)
        slope = 2.0 ** (-8.0 * (h + 1) / A_HEADS)
        s = _dot_nt(q_ref[:, sl], kw_ref[:, sl]) - slope * dist
        m = jnp.max(s, axis=-1, keepdims=True)
        p = jnp.exp(s - m)
        l = jnp.sum(p, axis=-1, keepdims=True)
        o_ref[:, sl] = (_dot(p.astype(BF16), vw_ref[:, sl]) / l).astype(o_ref.dtype)
        lse_all = jnp.where(lane == h, m + jnp.log(l), lse_all)
    lse_ref[...] = lse_all


def _dilated_branch(qkv):
    b, dil, sub, _ = qkv.shape
    tq = TQ_DIL
    nq = sub // tq
    nhalo = sub // N_SIDE
    per = tq // N_SIDE

    def own(which):
        return pl.BlockSpec((None, None, tq, A_WIDTH), lambda bi, r, i: (bi, r, i, which))

    def prev(which):
        return pl.BlockSpec((None, None, N_SIDE, A_WIDTH),
                            lambda bi, r, i: (bi, r, jnp.maximum(per * i - 1, 0), which))

    def nxt(which):
        return pl.BlockSpec((None, None, N_SIDE, A_WIDTH),
                            lambda bi, r, i: (bi, r, jnp.minimum(per * (i + 1), nhalo - 1), which))

    return pl.pallas_call(
        functools.partial(_dilated_body, dil=dil, nq=nq),
        grid=(b, dil, nq),
        in_specs=[own(0), own(1), prev(1), nxt(1), own(2), prev(2), nxt(2)],
        out_specs=[
            pl.BlockSpec((None, None, tq, A_WIDTH), lambda bi, r, i: (bi, r, i, 0)),
            pl.BlockSpec((None, None, tq, LANES), lambda bi, r, i: (bi, r, i, 0)),
        ],
        out_shape=[
            jax.ShapeDtypeStruct((b, dil, sub, A_WIDTH), BF16),
            jax.ShapeDtypeStruct((b, dil, sub, LANES), F32),
        ],
        scratch_shapes=[
            pltpu.VMEM((tq + 2 * N_SIDE, A_WIDTH), BF16),
            pltpu.VMEM((tq + 2 * N_SIDE, A_WIDTH), BF16),
        ],
        compiler_params=_cparams(("parallel", "parallel", "arbitrary")),
    )(qkv, qkv, qkv, qkv, qkv, qkv, qkv)


def _outproj_body(*refs):
    nd = len(DILATIONS)
    o_refs, l_refs = refs[:nd], refs[nd:2 * nd]
    ob_ref, ga_ref, gb_ref, w_ref, x_ref, y_ref, mix_ref, oa_ref, t_ref, lt_ref = refs[2 * nd:]
    j = pl.program_id(1)
    tm = x_ref.shape[0]

    def rows_of(d, r):
        return slice(None) if d == 1 else pl.ds(r, tm // d, stride=d)

    @pl.when(j == 0)
    def _():
        for i, d in enumerate(DILATIONS):
            for r in range(d):
                lt_ref.at[i][rows_of(d, r), :] = l_refs[i][r]
        ls = [lt_ref[i] for i in range(nd)]
        mx = functools.reduce(jnp.maximum, ls)
        es = [jnp.exp(l - mx) for l in ls]
        inv = 1.0 / functools.reduce(lambda a, b: a + b, es)
        head_of_col = lax.broadcasted_iota(jnp.int32, (2 * LANES, A_WIDTH), 1) // A_HEAD_DIM
        lane_of_row = lax.broadcasted_iota(jnp.int32, (2 * LANES, A_WIDTH), 0) % LANES
        expand = (head_of_col == lane_of_row).astype(BF16)
        for i, d in enumerate(DILATIONS):
            w = es[i] * inv
            hi = w.astype(BF16)
            lo = (w - hi.astype(F32)).astype(BF16)
            wexp = _dot(jnp.concatenate([hi, lo], axis=1), expand)
            for h in range(A_HEADS):
                sl = slice(h * A_HEAD_DIM, (h + 1) * A_HEAD_DIM)
                for r in range(d):
                    t_ref.at[h][rows_of(d, r), :] = o_refs[i][r, :, sl].astype(F32)
                if i == 0:
                    oa_ref[h] = wexp[:, sl] * t_ref[h]
                else:
                    oa_ref[h] += wexp[:, sl] * t_ref[h]
        ss = jnp.zeros((tm, 1), F32)
        for h in range(A_HEADS):
            oa = oa_ref[h]
            ss = ss + jnp.sum(oa * oa, axis=-1, keepdims=True)
        inv_a = lax.rsqrt(ss * (1.0 / A_WIDTH) + EPS)
        for h in range(A_HEADS):
            sl = slice(h * A_HEAD_DIM, (h + 1) * A_HEAD_DIM)
            mix_ref[:, sl] = (oa_ref[h] * inv_a * ga_ref[:, sl]).astype(BF16)
        mix_ref[:, A_WIDTH:] = _rms(ob_ref[...].astype(F32), gb_ref[...]).astype(BF16)

    y_ref[...] = x_ref[...] + _dot(mix_ref[...], w_ref[...])


def _outproj(branches, ob, ga, gb, w_o, x, seq):
    n = x.shape[0]
    tm = min(TM_OUT, seq)
    tn = TN_OUT
    nseq = seq // tm
    row = lambda i, j: (i, 0)
    const = lambda i, j: (0, 0)
    mixw = A_WIDTH + MLA_WIDTH
    res = lambda i, j: (i // nseq, 0, i % nseq, 0)
    o_specs = [pl.BlockSpec((None, d, tm // d, A_WIDTH), res) for d in DILATIONS]
    l_specs = [pl.BlockSpec((None, d, tm // d, LANES), res) for d in DILATIONS]
    return pl.pallas_call(
        _outproj_body,
        grid=(n // tm, D_MODEL // tn),
        in_specs=o_specs + l_specs + [
            pl.BlockSpec((tm, MLA_WIDTH), row),
            pl.BlockSpec((1, A_WIDTH), const), pl.BlockSpec((1, MLA_WIDTH), const),
            pl.BlockSpec((mixw, tn), lambda i, j: (0, j)),
            pl.BlockSpec((tm, tn), lambda i, j: (i, j)),
        ],
        out_specs=pl.BlockSpec((tm, tn), lambda i, j: (i, j)),
        out_shape=jax.ShapeDtypeStruct((n, D_MODEL), F32),
        scratch_shapes=[
            pltpu.VMEM((tm, mixw), BF16),
            pltpu.VMEM((A_HEADS, tm, A_HEAD_DIM), F32),
            pltpu.VMEM((A_HEADS, tm, A_HEAD_DIM), F32),
            pltpu.VMEM((len(DILATIONS), tm, LANES), F32),
        ],
        compiler_params=_cparams(("parallel", "arbitrary")),
    )(*[o for o, _ in branches], *[l for _, l in branches], ob, ga, gb, w_o, x)


def _router_body(x_ref, g_ref, wr_ref, h_ref, aff_ref):
    h = _rms(x_ref[...], g_ref[...])
    h_ref[...] = h
    logits = lax.dot_general(wr_ref[...], h, (((1,), (1,)), ((), ())), preferred_element_type=F32,
                             precision=lax.Precision.HIGHEST)
    m = jnp.max(logits, axis=0, keepdims=True)
    e = jnp.exp(logits - m)
    aff_ref[...] = e / jnp.sum(e, axis=0, keepdims=True)


def _router(x2, g, wr_t):
    n = x2.shape[0]
    tm = min(TM_ROUTER, n)
    return pl.pallas_call(
        _router_body,
        grid=(n // tm,),
        in_specs=[
            pl.BlockSpec((tm, D_MODEL), lambda i: (i, 0)),
            pl.BlockSpec((1, D_MODEL), lambda i: (0, 0)),
            pl.BlockSpec((N_EXPERTS, D_MODEL), lambda i: (0, 0)),
        ],
        out_specs=[
            pl.BlockSpec((tm, D_MODEL), lambda i: (i, 0)),
            pl.BlockSpec((N_EXPERTS, tm), lambda i: (0, i)),
        ],
        out_shape=[
            jax.ShapeDtypeStruct((n, D_MODEL), F32),
            jax.ShapeDtypeStruct((N_EXPERTS, n), F32),
        ],
        compiler_params=_cparams(("parallel",)),
    )(x2, g, wr_t)


def _sum_all(x):
    return jnp.sum(jnp.sum(x, axis=1, keepdims=True), axis=0, keepdims=True)


def _topk_body(a_ref, idx_ref, gate_ref, *, cap):
    a = a_ref[...]
    nc = a.shape[0]
    bits = pltpu.bitcast(a, jnp.int32)
    thr = jnp.zeros((1, 1), jnp.int32)
    for bit in range(30, -1, -1):
        cand = thr + (1 << bit)
        cnt = _sum_all((bits >= cand).astype(F32))
        thr = jnp.where(cnt >= cap, cand, thr)
    gt = bits > thr
    eq = bits == thr
    eqf = eq.astype(F32)
    need = cap - _sum_all(gt.astype(F32))

    li = lax.broadcasted_iota(jnp.int32, (LANES, LANES), 0)
    lj = lax.broadcasted_iota(jnp.int32, (LANES, LANES), 1)
    upper_incl = (li <= lj).astype(BF16)
    ci = lax.broadcasted_iota(jnp.int32, (nc, nc), 0)
    cj = lax.broadcasted_iota(jnp.int32, (nc, nc), 1)
    lower_strict = (cj < ci).astype(BF16)
    upper_incl_c = (ci <= cj).astype(BF16)

    p_eq = _dot(eqf.astype(BF16), upper_incl)
    tot_eq = jnp.broadcast_to(p_eq[:, LANES - 1:LANES], (nc, LANES))
    off_eq = _dot(lower_strict, tot_eq.astype(BF16))
    sel = gt | (eq & ((p_eq - eqf + off_eq) < need))
    selb = sel.astype(F32).astype(BF16)

    p_sel = _dot(selb, upper_incl)
    tot_row = _dot_nt(jnp.ones((8, LANES), BF16), selb)
    offi_row = _dot(tot_row.astype(BF16), upper_incl_c)[0:1, :]
    offe_row = offi_row - tot_row[0:1, :]

    slot_c = lax.broadcasted_iota(jnp.int32, (cap, nc), 0).astype(F32)
    chunk_of = jnp.sum((offi_row <= slot_c).astype(F32), axis=1, keepdims=True)
    onehot = lax.broadcasted_iota(jnp.int32, (cap, nc), 1).astype(F32) == chunk_of
    onehot_b = onehot.astype(F32).astype(BF16)
    run = _dot(onehot_b, p_sel.astype(BF16))
    offe = jnp.sum(jnp.where(onehot, offe_row, 0.0), axis=1, keepdims=True)
    slot_l = lax.broadcasted_iota(jnp.int32, (cap, LANES), 0).astype(F32)
    lane_of = jnp.sum(((run + offe) <= slot_l).astype(F32), axis=1, keepdims=True)
    idx_ref[...] = (chunk_of * LANES + lane_of).astype(jnp.int32)

    a1 = a.astype(BF16)
    r1 = a - a1.astype(F32)
    a2 = r1.astype(BF16)
    a3 = (r1 - a2.astype(F32)).astype(BF16)
    rows = _dot(onehot_b, a1) + _dot(onehot_b, a2) + _dot(onehot_b, a3)
    lanes = lax.broadcasted_iota(jnp.int32, (cap, LANES), 1).astype(F32)
    gate_ref[...] = jnp.sum(jnp.where(lanes == lane_of, rows, 0.0), axis=1, keepdims=True)


def _topk(aff_t, cap):
    e, n = aff_t.shape
    nc = n // LANES
    return pl.pallas_call(
        functools.partial(_topk_body, cap=cap),
        grid=(e,),
        in_specs=[pl.BlockSpec((None, nc, LANES), lambda i: (i, 0, 0))],
        out_specs=[
            pl.BlockSpec((None, cap, 1), lambda i: (i, 0, 0)),
            pl.BlockSpec((None, cap, 1), lambda i: (i, 0, 0)),
        ],
        out_shape=[
            jax.ShapeDtypeStruct((e, cap, 1), jnp.int32),
            jax.ShapeDtypeStruct((e, cap, 1), F32),
        ],
        compiler_params=_cparams(("parallel",)),
    )(aff_t.reshape(e, nc, LANES))


def _row_copy(src_hbm, tok, dst_ref, r, sem):
    return pltpu.make_async_copy(src_hbm.at[pl.ds(tok, 1), :], dst_ref.at[pl.ds(r, 1), :], sem)


def _moe_up_body(idx_ref, h_hbm, gate_ref, wg_ref, wu_ref, hid_ref, xe_ref, stage_ref, sem, *, cap, rows):
    e = pl.program_id(0)
    f = pl.program_id(1)

    @pl.when(f == 0)
    def _():
        nchunk = cap // rows

        def start(c, slot):
            def body(r, carry):
                tok = idx_ref[e * cap + c * rows + r]
                _row_copy(h_hbm, tok, stage_ref.at[slot], r, sem.at[slot]).start()
                return carry
            lax.fori_loop(0, rows, body, 0, unroll=DMA_UNROLL)

        def wait(slot):
            pltpu.make_async_copy(h_hbm.at[pl.ds(0, rows), :], stage_ref.at[slot], sem.at[slot]).wait()

        start(0, 0)
        for c in range(nchunk):
            slot = c % 2
            if c + 1 < nchunk:
                start(c + 1, 1 - slot)
            wait(slot)
            xe_ref[c * rows:(c + 1) * rows, :] = stage_ref[slot].astype(BF16)

    xe = xe_ref[...]
    g = _dot(xe, wg_ref[...].astype(BF16))
    u = _dot(xe, wu_ref[...].astype(BF16))
    hid = g / (1.0 + jnp.exp(-g)) * u * gate_ref[...]
    hid_ref[...] = hid.astype(BF16)


def _moe_up(idx_flat, h2, gate, w_gate, w_up, cap):
    n = h2.shape[0]
    tf = TF_MOE
    rows = min(G_GATHER, cap)
    grid_spec = pltpu.PrefetchScalarGridSpec(
        num_scalar_prefetch=1,
        grid=(N_EXPERTS, EXPERT_FF // tf),
        in_specs=[
            pl.BlockSpec(memory_space=pl.ANY),
            pl.BlockSpec((None, cap, 1), lambda e, f, idx: (e, 0, 0)),
            pl.BlockSpec((None, D_MODEL, tf), lambda e, f, idx: (e, 0, f)),
            pl.BlockSpec((None, D_MODEL, tf), lambda e, f, idx: (e, 0, f)),
        ],
        out_specs=pl.BlockSpec((None, cap, tf), lambda e, f, idx: (e, 0, f)),
        scratch_shapes=[
            pltpu.VMEM((cap, D_MODEL), BF16),
            pltpu.VMEM((2, rows, D_MODEL), F32),
            pltpu.SemaphoreType.DMA((2,)),
        ],
    )
    return pl.pallas_call(
        functools.partial(_moe_up_body, cap=cap, rows=rows),
        grid_spec=grid_spec,
        out_shape=jax.ShapeDtypeStruct((N_EXPERTS, cap, EXPERT_FF), BF16),
        compiler_params=_cparams(("arbitrary", "arbitrary")),
    )(idx_flat, h2, gate, w_gate, w_up)


def _moe_down_body(idx_ref, hid_ref, wd_ref, x_hbm, out_hbm, acc_ref, res_ref, sem, *, cap, rows, nkt):
    del x_hbm
    e = pl.program_id(0)
    rb = pl.program_id(1)
    kt = pl.program_id(2)
    last_kt = pl.num_programs(2) - 1
    base = e * cap + rb * rows
    gather_sem, scatter_sem = sem.at[0], sem.at[1]

    def wait_scatter():
        pltpu.make_async_copy(res_ref, out_hbm.at[pl.ds(0, rows), :], scatter_sem).wait()

    @pl.when(kt == 0)
    def _():
        @pl.when((e > 0) | (rb > 0))
        def _():
            wait_scatter()

        acc_ref[...] = jnp.zeros(acc_ref.shape, F32)

    seg = rows // nkt
    for u in range(seg):
        r = kt * seg + u
        pltpu.make_async_copy(out_hbm.at[pl.ds(idx_ref[base + r], 1), :], res_ref.at[pl.ds(r, 1), :],
                              gather_sem).start()

    acc_ref[...] += _dot(hid_ref[...], wd_ref[...].astype(BF16))

    @pl.when(kt == last_kt)
    def _():
        pltpu.make_async_copy(out_hbm.at[pl.ds(0, rows), :], res_ref, gather_sem).wait()
        res_ref[...] += acc_ref[...]

        def start(r, carry):
            pltpu.make_async_copy(res_ref.at[pl.ds(r, 1), :], out_hbm.at[pl.ds(idx_ref[base + r], 1), :],
                                  scatter_sem).start()
            return carry
        lax.fori_loop(0, rows, start, 0, unroll=DMA_UNROLL)

        @pl.when((e == pl.num_programs(0) - 1) & (rb == pl.num_programs(1) - 1))
        def _():
            wait_scatter()


def _moe_down(idx_flat, hid, w_down, x2, cap):
    n = x2.shape[0]
    rows = min(R_DOWN, cap)
    tk = TK_DOWN
    grid_spec = pltpu.PrefetchScalarGridSpec(
        num_scalar_prefetch=1,
        grid=(N_EXPERTS, cap // rows, EXPERT_FF // tk),
        in_specs=[
            pl.BlockSpec((None, rows, tk), lambda e, rb, kt, idx: (e, rb, kt)),
            pl.BlockSpec((None, tk, D_MODEL), lambda e, rb, kt, idx: (e, kt, 0)),
            pl.BlockSpec(memory_space=pl.ANY),
        ],
        out_specs=pl.BlockSpec(memory_space=pl.ANY),
        scratch_shapes=[
            pltpu.VMEM((rows, D_MODEL), F32),
            pltpu.VMEM((rows, D_MODEL), F32),
            pltpu.SemaphoreType.DMA((2,)),
        ],
    )
    return pl.pallas_call(
        functools.partial(_moe_down_body, cap=cap, rows=rows, nkt=EXPERT_FF // tk),
        grid_spec=grid_spec,
        out_shape=jax.ShapeDtypeStruct((n, D_MODEL), F32),
        input_output_aliases={3: 0},
        compiler_params=_cparams(("arbitrary", "arbitrary", "arbitrary")),
    )(idx_flat, hid, w_down, x2)


def _pad_rope_cols(t):
    half = QK_ROPE // 2
    z = jnp.zeros(t.shape[:-1] + (half,), t.dtype)
    return jnp.concatenate([t[..., :half], z, t[..., half:], z], axis=-1)


def _rope_tables_padded(seq):
    inv = ROPE_THETA ** (-jnp.arange(0, QK_ROPE, 2, dtype=F32) / QK_ROPE)
    ang = jnp.arange(seq, dtype=F32)[:, None] * inv[None, :]
    cos, sin = jnp.cos(ang), jnp.sin(ang)
    z = jnp.zeros_like(cos)
    return jnp.concatenate([cos, z, cos, z], axis=-1), jnp.concatenate([-sin, z, sin, z], axis=-1)


def _prepare_weights(w_in, mla_w_q_b, mla_w_kv_b, w_o, w_router, mla_q_rope_norm_g, mla_k_rope_norm_g):
    qkv_cols = 3 * A_WIDTH
    w_a = w_in[:, :qkv_cols].astype(BF16)
    w_ql = w_in[:, qkv_cols:qkv_cols + Q_LORA].astype(BF16)
    w_kvl = w_in[:, qkv_cols + Q_LORA:qkv_cols + Q_LORA + KV_LORA].astype(BF16)
    w_kr = _pad_rope_cols(w_in[:, qkv_cols + Q_LORA + KV_LORA:]).astype(BF16)
    wq = mla_w_q_b.reshape(Q_LORA, MLA_HEADS, QK_NOPE + QK_ROPE)
    wq_pad = jnp.concatenate([wq[..., :QK_NOPE], _pad_rope_cols(wq[..., QK_NOPE:])], axis=-1)
    wq_pad = wq_pad.reshape(Q_LORA, MLA_HEADS * MLA_HEAD_PAD).astype(BF16)
    return dict(
        w_a=w_a, w_ql=w_ql, w_kvl=w_kvl, w_kr=w_kr, wq_pad=wq_pad,
        wkv=mla_w_kv_b.astype(BF16), w_o=w_o.astype(BF16), wr_t=w_router.T,
        gqr_pad=_pad_rope_cols(mla_q_rope_norm_g)[None, :], gkr_pad=_pad_rope_cols(mla_k_rope_norm_g)[None, :],
    )


def _encoder_layer(x3, pw, norm_mix_g, a_q_norm_g, a_k_norm_g, mla_q_a_norm_g, mla_kv_a_norm_g,
                   mla_q_nope_norm_g, mla_k_nope_norm_g, out_norm_a_g, out_norm_b_g, norm_ffn_g,
                   w_gate, w_up, w_down):
    b, s, d = x3.shape
    n = b * s
    x = x3.reshape(n, d)
    row = lambda v: v[None, :]
    cos_pad, sin_pad = _rope_tables_padded(s)

    qkv_by_dilation = _inproj_a(x, row(norm_mix_g), pw["w_a"], row(a_q_norm_g), row(a_k_norm_g), b, s)
    branches = [_dilated_branch(qkv) for qkv in qkv_by_dilation]

    cq, ckv, kr = _inproj_lat(x, row(norm_mix_g), pw["w_ql"], pw["w_kvl"], pw["w_kr"], row(mla_q_a_norm_g),
                              row(mla_kv_a_norm_g), pw["gkr_pad"], cos_pad, sin_pad, s)
    qm, km, vm = _mla_up(cq, ckv, kr, pw["wq_pad"], pw["wkv"], row(mla_q_nope_norm_g), pw["gqr_pad"],
                         row(mla_k_nope_norm_g), cos_pad, sin_pad, s)
    wide = MLA_HEADS * MLA_HEAD_PAD
    ob = _mla_flash(qm.reshape(b, s, wide), km.reshape(b, s, wide), vm.reshape(b, s, wide))

    x2 = _outproj(branches, ob.reshape(n, MLA_WIDTH), row(out_norm_a_g), row(out_norm_b_g), pw["w_o"], x, s)

    h2, aff_t = _router(x2, row(norm_ffn_g), pw["wr_t"])
    cap = CAPACITY_FACTOR * n // N_EXPERTS
    idx, gate = _topk(aff_t, cap)
    idx_flat = idx.reshape(N_EXPERTS * cap)
    hid = _moe_up(idx_flat, h2, gate, w_gate, w_up, cap)
    y = _moe_down(idx_flat, hid, w_down, x2, cap)
    return y.reshape(b, s, d)


def kernel(x_prompt, x_sample, norm_mix_g, w_in, a_q_norm_g, a_k_norm_g, mla_q_a_norm_g, mla_w_q_b, mla_kv_a_norm_g, mla_w_kv_b, mla_q_nope_norm_g, mla_q_rope_norm_g, mla_k_nope_norm_g, mla_k_rope_norm_g, out_norm_a_g, out_norm_b_g, w_o, norm_ffn_g, w_router, w_gate, w_up, w_down):
    depth = w_in.shape[0]
    prepared = [_prepare_weights(w_in[l], mla_w_q_b[l], mla_w_kv_b[l], w_o[l], w_router[l],
                                 mla_q_rope_norm_g[l], mla_k_rope_norm_g[l]) for l in range(depth)]

    def run_trunk(x):
        for l in range(depth):
            pw = prepared[l]
            x = _encoder_layer(x, pw, norm_mix_g[l], a_q_norm_g[l], a_k_norm_g[l], mla_q_a_norm_g[l],
                               mla_kv_a_norm_g[l], mla_q_nope_norm_g[l], mla_k_nope_norm_g[l],
                               out_norm_a_g[l], out_norm_b_g[l], norm_ffn_g[l], w_gate[l], w_up[l], w_down[l])
        return x

    return run_trunk(x_prompt), run_trunk(x_sample)
```

```python
import functools
import math

import jax
import jax.numpy as jnp
from jax import lax
from jax.experimental import pallas as pl
from jax.experimental.pallas import tpu as pltpu

F32 = jnp.float32
BF16 = jnp.bfloat16

D_MODEL = 4096
A_HEADS = 16
A_HEAD_DIM = 128
A_WIDTH = A_HEADS * A_HEAD_DIM
DILATED_PATTERNS = ((128, 1), (512, 4), (2048, 16))
DILATIONS = tuple(d for _, d in DILATED_PATTERNS)
MLA_HEADS = 16
QK_NOPE = 128
QK_ROPE = 64
V_HEAD = 128
Q_LORA = 1024
KV_LORA = 512
MLA_WIDTH = MLA_HEADS * V_HEAD
ROPE_THETA = 10000.0
N_EXPERTS = 16
EXPERT_FF = 2048
CAPACITY_FACTOR = 2
EPS = 1e-6

LANES = 128
MLA_HEAD_PAD = 2 * LANES
N_SIDE = 64

TM_INPROJ = 512
TN_INPROJ = 512
TM_LAT = 256
TM_MLAUP = 256
TQ_MLA = 1024
TK_MLA = 2048
TQ_DIL = 128
TM_OUT = 512
TN_OUT = 512
TM_ROUTER = 256
TF_MOE = 256
ROW_CHUNK_UP = 512
TK_DOWN = 256
R_DOWN = 1024
G_GATHER = 256
DMA_UNROLL = 8
VMEM_LIMIT = 56 * 1024 * 1024


def _cparams(sem):
    return pltpu.CompilerParams(dimension_semantics=sem, vmem_limit_bytes=VMEM_LIMIT)


def _rms(x, g):
    ms = jnp.mean(x * x, axis=-1, keepdims=True)
    return x * lax.rsqrt(ms + EPS) * g


def _dot(a, b):
    return jnp.dot(a, b, preferred_element_type=F32)


def _dot_nt(a, b):
    return lax.dot_general(a, b, (((1,), (1,)), ((), ())), preferred_element_type=F32)


def _inproj_a_body(x_ref, g_ref, w_ref, gq_ref, gk_ref, *refs, chunk, ncol_tiles):
    nd = len(DILATIONS)
    out_refs = refs[:nd]
    xn_ref, y0_ref, y1_ref, z_ref = refs[nd:]
    t = pl.program_id(0)
    j = t % ncol_tiles
    jp = (t + ncol_tiles - 1) % ncol_tiles
    tm = x_ref.shape[0]
    tn = w_ref.shape[1]
    nq = A_WIDTH // tn
    planes = tn // LANES

    @pl.when(t == 0)
    def _():
        y1_ref[...] = jnp.zeros(y1_ref.shape, F32)

    @pl.when(j == 0)
    def _():
        def norm_chunk(c, carry):
            r = pl.multiple_of(c * chunk, chunk)
            xn_ref[pl.ds(r, chunk), :] = _rms(x_ref[pl.ds(r, chunk), :], g_ref[...]).astype(BF16)
            return carry

        lax.fori_loop(0, tm // chunk, norm_chunk, 0)

    is_q = jp < nq
    is_k = (jp >= nq) & (jp < 2 * nq)
    q_gain = gq_ref[...] * (A_HEAD_DIM ** -0.5 * math.log2(math.e))
    gain = jnp.where(is_q, q_gain, jnp.where(is_k, gk_ref[...], jnp.ones_like(q_gain)))

    def work(y_cur, y_prev):
        y_cur[...] = _dot(xn_ref[...], w_ref[...])
        for c in range(planes):
            yh = y_prev[:, c * LANES:(c + 1) * LANES]
            ms = jnp.mean(yh * yh, axis=-1, keepdims=True)
            factor = jnp.where(is_q | is_k, lax.rsqrt(ms + EPS), jnp.ones_like(ms))
            z_ref[c] = yh * factor * gain
        for o_ref, d in zip(out_refs, DILATIONS):
            for c in range(planes):
                cols = slice(c * LANES, (c + 1) * LANES)
                if d == 1:
                    o_ref[:, cols] = z_ref[c].astype(BF16)
                else:
                    for r in range(d):
                        o_ref[r, :, cols] = z_ref.at[c][pl.ds(r, tm // d, stride=d), :].astype(BF16)

    @pl.when(t % 2 == 0)
    def _():
        work(y0_ref, y1_ref)

    @pl.when(t % 2 == 1)
    def _():
        work(y1_ref, y0_ref)


def _inproj_a(x, g, w_a, gq, gk, batch, seq):
    n = x.shape[0]
    tm = min(TM_INPROJ, seq)
    tn = TN_INPROJ
    ncol = w_a.shape[1]
    nseq = seq // tm
    nrow_tiles = n // tm
    nct = ncol // tn
    steps = nrow_tiles * nct + 1

    def prev_tile(t):
        tp = jnp.maximum(t - 1, 0)
        return tp // nct, tp % nct

    def residue_major(t):
        i, j = prev_tile(t)
        return i // nseq, 0, i % nseq, j

    out_specs, out_shape = [], []
    for d in DILATIONS:
        if d == 1:
            out_specs.append(pl.BlockSpec((tm, tn), prev_tile))
            out_shape.append(jax.ShapeDtypeStruct((n, ncol), BF16))
        else:
            out_specs.append(pl.BlockSpec((None, d, tm // d, tn), residue_major))
            out_shape.append(jax.ShapeDtypeStruct((batch, d, seq // d, ncol), BF16))
    outs = pl.pallas_call(
        functools.partial(_inproj_a_body, chunk=128, ncol_tiles=nct),
        grid=(steps,),
        in_specs=[
            pl.BlockSpec((tm, D_MODEL), lambda t: (jnp.minimum(t // nct, nrow_tiles - 1), 0)),
            pl.BlockSpec((1, D_MODEL), lambda t: (0, 0)),
            pl.BlockSpec((D_MODEL, tn), lambda t: (0, t % nct)),
            pl.BlockSpec((1, A_HEAD_DIM), lambda t: (0, 0)),
            pl.BlockSpec((1, A_HEAD_DIM), lambda t: (0, 0)),
        ],
        out_specs=out_specs,
        out_shape=out_shape,
        scratch_shapes=[
            pltpu.VMEM((tm, D_MODEL), BF16),
            pltpu.VMEM((tm, tn), F32),
            pltpu.VMEM((tm, tn), F32),
            pltpu.VMEM((tn // LANES, tm, LANES), F32),
        ],
        compiler_params=_cparams(("arbitrary",)),
    )(x, g, w_a, gq, gk)
    return [o.reshape(batch, d, seq // d, ncol) for o, d in zip(outs, DILATIONS)]


def _rope_padded(t, cos, sin):
    return t * cos + pltpu.roll(t, 2 * (QK_ROPE // 2), 1) * sin


def _inproj_lat_body(x_ref, g_ref, wq_ref, wkv_ref, wkr_ref, gq_ref, gkv_ref, gkr_ref, cos_ref, sin_ref,
                     cq_ref, ckv_ref, kr_ref):
    xn = _rms(x_ref[...], g_ref[...]).astype(BF16)
    cq_ref[...] = _rms(_dot(xn, wq_ref[...]), gq_ref[...]).astype(BF16)
    ckv_ref[...] = _rms(_dot(xn, wkv_ref[...]), gkv_ref[...]).astype(BF16)
    kr = _dot(xn, wkr_ref[...])
    ms = jnp.sum(kr * kr, axis=-1, keepdims=True) * (1.0 / QK_ROPE)
    krn = kr * lax.rsqrt(ms + EPS) * gkr_ref[...]
    kr_ref[...] = _rope_padded(krn, cos_ref[...], sin_ref[...]).astype(BF16)


def _inproj_lat(x, g, wq, wkv, wkr, gq, gkv, gkr, cos_pad, sin_pad, seq):
    n = x.shape[0]
    tm = min(TM_LAT, seq)
    nseq = seq // tm
    const = lambda i: (0, 0)
    return pl.pallas_call(
        _inproj_lat_body,
        grid=(n // tm,),
        in_specs=[
            pl.BlockSpec((tm, D_MODEL), lambda i: (i, 0)),
            pl.BlockSpec((1, D_MODEL), const),
            pl.BlockSpec((D_MODEL, Q_LORA), const),
            pl.BlockSpec((D_MODEL, KV_LORA), const),
            pl.BlockSpec((D_MODEL, LANES), const),
            pl.BlockSpec((1, Q_LORA), const),
            pl.BlockSpec((1, KV_LORA), const),
            pl.BlockSpec((1, LANES), const),
            pl.BlockSpec((tm, LANES), lambda i: (i % nseq, 0)),
            pl.BlockSpec((tm, LANES), lambda i: (i % nseq, 0)),
        ],
        out_specs=[
            pl.BlockSpec((tm, Q_LORA), lambda i: (i, 0)),
            pl.BlockSpec((tm, KV_LORA), lambda i: (i, 0)),
            pl.BlockSpec((tm, LANES), lambda i: (i, 0)),
        ],
        out_shape=[
            jax.ShapeDtypeStruct((n, Q_LORA), BF16),
            jax.ShapeDtypeStruct((n, KV_LORA), BF16),
            jax.ShapeDtypeStruct((n, LANES), BF16),
        ],
        compiler_params=_cparams(("parallel",)),
    )(x, g, wq, wkv, wkr, gq, gkv, gkr, cos_pad, sin_pad)


def _mla_up_body(cq_ref, ckv_ref, kr_ref, wq_ref, wkv_ref, gqn_ref, gqr_ref, gkn_ref, cos_ref, sin_ref,
                 q_ref, k_ref, v_ref):
    cq = cq_ref[...]
    ckv = ckv_ref[...]
    kr = kr_ref[...]
    cos = cos_ref[...]
    sin = sin_ref[...]
    scale = (QK_NOPE + QK_ROPE) ** -0.5 * math.log2(math.e)
    ones = jnp.ones((cq.shape[0], V_HEAD), BF16)
    for h in range(MLA_HEADS):
        lo = h * MLA_HEAD_PAD
        q = _dot(cq, wq_ref[:, lo:lo + MLA_HEAD_PAD])
        qn = _rms(q[:, :LANES], gqn_ref[...] * scale)
        qr = q[:, LANES:]
        ms = jnp.sum(qr * qr, axis=-1, keepdims=True) * (1.0 / QK_ROPE)
        qr = _rope_padded(qr * lax.rsqrt(ms + EPS) * (gqr_ref[...] * scale), cos, sin)
        q_ref[:, lo:lo + LANES] = qn.astype(BF16)
        q_ref[:, lo + LANES:lo + MLA_HEAD_PAD] = qr.astype(BF16)
        kv = _dot(ckv, wkv_ref[:, lo:lo + MLA_HEAD_PAD])
        k_ref[:, lo:lo + LANES] = _rms(kv[:, :LANES], gkn_ref[...]).astype(BF16)
        k_ref[:, lo + LANES:lo + MLA_HEAD_PAD] = kr
        v_ref[:, lo:lo + V_HEAD] = kv[:, LANES:].astype(BF16)
        v_ref[:, lo + V_HEAD:lo + 2 * V_HEAD] = ones


def _mla_up(cq, ckv, kr, wq_pad, wkv, gqn, gqr_pad, gkn, cos_pad, sin_pad, seq):
    n = cq.shape[0]
    tm = min(TM_MLAUP, seq)
    nseq = seq // tm
    const = lambda i: (0, 0)
    row = lambda i: (i, 0)
    wide = MLA_HEADS * MLA_HEAD_PAD
    return pl.pallas_call(
        _mla_up_body,
        grid=(n // tm,),
        in_specs=[
            pl.BlockSpec((tm, Q_LORA), row),
            pl.BlockSpec((tm, KV_LORA), row),
            pl.BlockSpec((tm, LANES), row),
            pl.BlockSpec((Q_LORA, wide), const),
            pl.BlockSpec((KV_LORA, wide), const),
            pl.BlockSpec((1, LANES), const),
            pl.BlockSpec((1, LANES), const),
            pl.BlockSpec((1, LANES), const),
            pl.BlockSpec((tm, LANES), lambda i: (i % nseq, 0)),
            pl.BlockSpec((tm, LANES), lambda i: (i % nseq, 0)),
        ],
        out_specs=[
            pl.BlockSpec((tm, wide), row),
            pl.BlockSpec((tm, wide), row),
            pl.BlockSpec((tm, wide), row),
        ],
        out_shape=[
            jax.ShapeDtypeStruct((n, wide), BF16),
            jax.ShapeDtypeStruct((n, wide), BF16),
            jax.ShapeDtypeStruct((n, wide), BF16),
        ],
        compiler_params=_cparams(("parallel",)),
    )(cq, ckv, kr, wq_pad, wkv, gqn, gqr_pad, gkn, cos_pad, sin_pad)


def _mla_flash_body(q_ref, k_ref, v_ref, o_ref, m_ref, acc_ref, s0_ref, s1_ref, *, tk):
    nk = k_ref.shape[0] // tk
    m_ref[...] = jnp.full(m_ref.shape, -jnp.inf, F32)
    acc_ref[...] = jnp.zeros(acc_ref.shape, F32)

    def scores(c, s_ref):
        r = pl.multiple_of(c * tk, tk)
        s_ref[...] = _dot_nt(q_ref[...], k_ref[pl.ds(r, tk), :])

    def update(c, s_ref):
        r = pl.multiple_of(c * tk, tk)
        s = s_ref[...]
        m_old = m_ref[...]
        m_new = jnp.maximum(m_old, jnp.max(s, axis=-1, keepdims=True))
        alpha = jnp.exp2(m_old - m_new)
        p = jnp.exp2((s - jnp.tile(m_new, (1, tk // LANES))).astype(BF16))
        acc_ref[...] = jnp.tile(alpha, (1, 2)) * acc_ref[...] + _dot(p, v_ref[pl.ds(r, tk), :])
        m_ref[...] = m_new

    scores(0, s0_ref)

    def pair(jj, carry):
        c = 2 * jj
        scores(c + 1, s1_ref)
        update(c, s0_ref)
        scores(c + 2, s0_ref)
        update(c + 1, s1_ref)
        return carry

    lax.fori_loop(0, nk // 2 - 1, pair, 0)
    scores(nk - 1, s1_ref)
    update(nk - 2, s0_ref)
    update(nk - 1, s1_ref)
    o_ref[...] = (acc_ref[:, :V_HEAD] / acc_ref[:, V_HEAD:]).astype(o_ref.dtype)


def _mla_flash(qm, km, vm):
    b, s, _ = qm.shape
    tq = min(TQ_MLA, s)
    tk = min(TK_MLA, s // 2)
    return pl.pallas_call(
        functools.partial(_mla_flash_body, tk=tk),
        grid=(b, MLA_HEADS, s // tq),
        in_specs=[
            pl.BlockSpec((None, tq, MLA_HEAD_PAD), lambda bi, h, i: (bi, i, h)),
            pl.BlockSpec((None, s, MLA_HEAD_PAD), lambda bi, h, i: (bi, 0, h)),
            pl.BlockSpec((None, s, 2 * V_HEAD), lambda bi, h, i: (bi, 0, h)),
        ],
        out_specs=pl.BlockSpec((None, tq, V_HEAD), lambda bi, h, i: (bi, i, h)),
        out_shape=jax.ShapeDtypeStruct((b, s, MLA_WIDTH), BF16),
        scratch_shapes=[pltpu.VMEM((tq, LANES), F32), pltpu.VMEM((tq, 2 * V_HEAD), F32),
                        pltpu.VMEM((tq, tk), F32), pltpu.VMEM((tq, tk), F32)],
        compiler_params=_cparams(("parallel", "parallel", "arbitrary")),
    )(qm, km, vm)


def _dilated_body(q_ref, k0_ref, kp_ref, kn_ref, v0_ref, vp_ref, vn_ref, o_ref, lse_ref, kw_ref, vw_ref,
                  s_ref, *, dil, nq):
    i = pl.program_id(2)
    tq = q_ref.shape[0]
    win = tq + 2 * N_SIDE
    kw_ref[0:N_SIDE, :] = kp_ref[...]
    kw_ref[N_SIDE:N_SIDE + tq, :] = k0_ref[...]
    kw_ref[N_SIDE + tq:win, :] = kn_ref[...]
    vw_ref[0:N_SIDE, :] = vp_ref[...]
    vw_ref[N_SIDE:N_SIDE + tq, :] = v0_ref[...]
    vw_ref[N_SIDE + tq:win, :] = vn_ref[...]

    @pl.when(i == nq - 1)
    def _():
        kw_ref[N_SIDE + tq:win, :] = jnp.zeros((N_SIDE, A_WIDTH), BF16)
        vw_ref[N_SIDE + tq:win, :] = jnp.zeros((N_SIDE, A_WIDTH), BF16)

    row = lax.broadcasted_iota(jnp.int32, (tq, win), 0)
    col = lax.broadcasted_iota(jnp.int32, (tq, win), 1)
    rel = col - N_SIDE - row
    arel = jnp.abs(rel)
    lower_ok = (rel >= 0) | (row >= N_SIDE) | (i > 0)
    upper_ok = (rel < N_SIDE) | (row < tq - N_SIDE) | (i < nq - 1)
    valid = (arel <= N_SIDE) & lower_ok & upper_ok
    dist = jnp.where(valid, (dil * arel).astype(F32), jnp.inf)

    lane = lax.broadcasted_iota(jnp.int32, (tq, LANES), 1)
    lse_all = jnp.zeros((tq, LANES), F32)
    ones = jnp.ones((win, LANES), BF16)
    for h in range(A_HEADS):
        sl = slice(h * A_HEAD_DIM, (h + 1) * A_HEAD_DIM)
        s_ref[h] = _dot_nt(q_ref[:, sl], kw_ref[:, sl])
    for h in range(A_HEADS):
        sl = slice(h * A_HEAD_DIM, (h + 1) * A_HEAD_DIM)
        slope = 2.0 ** (-8.0 * (h + 1) / A_HEADS) * math.log2(math.e)
        s = s_ref[h] - slope * dist
        m = jnp.max(s, axis=-1, keepdims=True)
        p = jnp.exp2((s - m).astype(BF16))
        l = _dot(p, ones)
        o_ref[:, sl] = (_dot(p, vw_ref[:, sl]) / l).astype(o_ref.dtype)
        lse_all = jnp.where(lane == h, m + jnp.log2(l), lse_all)
    lse_ref[...] = lse_all


def _dilated_branch(qkv):
    b, dil, sub, _ = qkv.shape
    tq = TQ_DIL
    nq = sub // tq
    nhalo = sub // N_SIDE
    per = tq // N_SIDE

    def own(which):
        return pl.BlockSpec((None, None, tq, A_WIDTH), lambda bi, r, i: (bi, r, i, which))

    def prev(which):
        return pl.BlockSpec((None, None, N_SIDE, A_WIDTH),
                            lambda bi, r, i: (bi, r, jnp.maximum(per * i - 1, 0), which))

    def nxt(which):
        return pl.BlockSpec((None, None, N_SIDE, A_WIDTH),
                            lambda bi, r, i: (bi, r, jnp.minimum(per * (i + 1), nhalo - 1), which))

    return pl.pallas_call(
        functools.partial(_dilated_body, dil=dil, nq=nq),
        grid=(b, dil, nq),
        in_specs=[own(0), own(1), prev(1), nxt(1), own(2), prev(2), nxt(2)],
        out_specs=[
            pl.BlockSpec((None, None, tq, A_WIDTH), lambda bi, r, i: (bi, r, i, 0)),
            pl.BlockSpec((None, None, tq, LANES), lambda bi, r, i: (bi, r, i, 0)),
        ],
        out_shape=[
            jax.ShapeDtypeStruct((b, dil, sub, A_WIDTH), BF16),
            jax.ShapeDtypeStruct((b, dil, sub, LANES), F32),
        ],
        scratch_shapes=[
            pltpu.VMEM((tq + 2 * N_SIDE, A_WIDTH), BF16),
            pltpu.VMEM((tq + 2 * N_SIDE, A_WIDTH), BF16),
            pltpu.VMEM((A_HEADS, tq, tq + 2 * N_SIDE), F32),
        ],
        compiler_params=_cparams(("parallel", "parallel", "arbitrary")),
    )(qkv, qkv, qkv, qkv, qkv, qkv, qkv)


def _outproj_body(*refs):
    nd = len(DILATIONS)
    o_refs, l_refs = refs[:nd], refs[nd:2 * nd]
    ob_ref, ga_ref, gb_ref, w_ref, x_ref, y_ref, mix_ref, oa_ref, t_ref, lt_ref = refs[2 * nd:]
    j = pl.program_id(1)
    tm = x_ref.shape[0]

    def rows_of(d, r):
        return slice(None) if d == 1 else pl.ds(r, tm // d, stride=d)

    @pl.when(j == 0)
    def _():
        for i, d in enumerate(DILATIONS):
            for r in range(d):
                lt_ref.at[i][rows_of(d, r), :] = l_refs[i][r]
        ls = [lt_ref[i] for i in range(nd)]
        mx = functools.reduce(jnp.maximum, ls)
        es = [jnp.exp2(l - mx) for l in ls]
        inv = 1.0 / functools.reduce(lambda a, b: a + b, es)
        head_of_col = lax.broadcasted_iota(jnp.int32, (2 * LANES, A_WIDTH), 1) // A_HEAD_DIM
        lane_of_row = lax.broadcasted_iota(jnp.int32, (2 * LANES, A_WIDTH), 0) % LANES
        expand = (head_of_col == lane_of_row).astype(BF16)
        for i, d in enumerate(DILATIONS):
            w = es[i] * inv
            hi = w.astype(BF16)
            lo = (w - hi.astype(F32)).astype(BF16)
            wexp = _dot(jnp.concatenate([hi, lo], axis=1), expand)
            for h in range(A_HEADS):
                sl = slice(h * A_HEAD_DIM, (h + 1) * A_HEAD_DIM)
                for r in range(d):
                    t_ref.at[h][rows_of(d, r), :] = o_refs[i][r, :, sl].astype(F32)
                if i == 0:
                    oa_ref[h] = wexp[:, sl] * t_ref[h]
                else:
                    oa_ref[h] += wexp[:, sl] * t_ref[h]
        ss = jnp.zeros((tm, 1), F32)
        for h in range(A_HEADS):
            oa = oa_ref[h]
            ss = ss + jnp.sum(oa * oa, axis=-1, keepdims=True)
        inv_a = lax.rsqrt(ss * (1.0 / A_WIDTH) + EPS)
        for h in range(A_HEADS):
            sl = slice(h * A_HEAD_DIM, (h + 1) * A_HEAD_DIM)
            mix_ref[:, sl] = (oa_ref[h] * inv_a * ga_ref[:, sl]).astype(BF16)
        mix_ref[:, A_WIDTH:] = _rms(ob_ref[...].astype(F32), gb_ref[...]).astype(BF16)

    y_ref[...] = x_ref[...] + _dot(mix_ref[...], w_ref[...])


def _outproj(branches, ob, ga, gb, w_o, x, seq):
    n = x.shape[0]
    tm = min(TM_OUT, seq)
    tn = TN_OUT
    nseq = seq // tm
    row = lambda i, j: (i, 0)
    const = lambda i, j: (0, 0)
    mixw = A_WIDTH + MLA_WIDTH
    res = lambda i, j: (i // nseq, 0, i % nseq, 0)
    o_specs = [pl.BlockSpec((None, d, tm // d, A_WIDTH), res) for d in DILATIONS]
    l_specs = [pl.BlockSpec((None, d, tm // d, LANES), res) for d in DILATIONS]
    return pl.pallas_call(
        _outproj_body,
        grid=(n // tm, D_MODEL // tn),
        in_specs=o_specs + l_specs + [
            pl.BlockSpec((tm, MLA_WIDTH), row),
            pl.BlockSpec((1, A_WIDTH), const), pl.BlockSpec((1, MLA_WIDTH), const),
            pl.BlockSpec((mixw, tn), lambda i, j: (0, j)),
            pl.BlockSpec((tm, tn), lambda i, j: (i, j)),
        ],
        out_specs=pl.BlockSpec((tm, tn), lambda i, j: (i, j)),
        out_shape=jax.ShapeDtypeStruct((n, D_MODEL), F32),
        scratch_shapes=[
            pltpu.VMEM((tm, mixw), BF16),
            pltpu.VMEM((A_HEADS, tm, A_HEAD_DIM), F32),
            pltpu.VMEM((A_HEADS, tm, A_HEAD_DIM), F32),
            pltpu.VMEM((len(DILATIONS), tm, LANES), F32),
        ],
        compiler_params=_cparams(("parallel", "arbitrary")),
    )(*[o for o, _ in branches], *[l for _, l in branches], ob, ga, gb, w_o, x)


def _router_body(x_ref, g_ref, w1_ref, w2_ref, h_ref, aff_ref):
    h = _rms(x_ref[...], g_ref[...])
    h_ref[...] = h
    h1 = h.astype(BF16)
    h2 = (h - h1.astype(F32)).astype(BF16)
    logits = _dot(h1, w1_ref[...]) + _dot(h1, w2_ref[...]) + _dot(h2, w1_ref[...])
    lane = lax.broadcasted_iota(jnp.int32, logits.shape, 1)
    logits = jnp.where(lane < N_EXPERTS, logits, -jnp.inf)
    m = jnp.max(logits, axis=-1, keepdims=True)
    e = jnp.exp(logits - m)
    aff_ref[...] = e / jnp.sum(e, axis=-1, keepdims=True)


def _router(x2, g, wr1, wr2):
    n = x2.shape[0]
    tm = min(TM_ROUTER, n)
    h2, aff = pl.pallas_call(
        _router_body,
        grid=(n // tm,),
        in_specs=[
            pl.BlockSpec((tm, D_MODEL), lambda i: (i, 0)),
            pl.BlockSpec((1, D_MODEL), lambda i: (0, 0)),
            pl.BlockSpec((D_MODEL, LANES), lambda i: (0, 0)),
            pl.BlockSpec((D_MODEL, LANES), lambda i: (0, 0)),
        ],
        out_specs=[
            pl.BlockSpec((tm, D_MODEL), lambda i: (i, 0)),
            pl.BlockSpec((tm, LANES), lambda i: (i, 0)),
        ],
        out_shape=[
            jax.ShapeDtypeStruct((n, D_MODEL), F32),
            jax.ShapeDtypeStruct((n, LANES), F32),
        ],
        compiler_params=_cparams(("parallel",)),
    )(x2, g, wr1, wr2)
    return h2, aff[:, :N_EXPERTS].T


def _sum_all(x):
    return jnp.sum(jnp.sum(x, axis=1, keepdims=True), axis=0, keepdims=True)


def _topk_body(a_ref, idx_ref, gate_ref, *, cap):
    a = a_ref[...]
    nc = a.shape[0]
    bits = pltpu.bitcast(a, jnp.int32)
    thr = jnp.zeros((1, 1), jnp.int32)
    for bit in range(30, -1, -1):
        cand = thr + (1 << bit)
        cnt = _sum_all((bits >= cand).astype(F32))
        thr = jnp.where(cnt >= cap, cand, thr)
    gt = bits > thr
    eq = bits == thr
    eqf = eq.astype(F32)
    need = cap - _sum_all(gt.astype(F32))

    li = lax.broadcasted_iota(jnp.int32, (LANES, LANES), 0)
    lj = lax.broadcasted_iota(jnp.int32, (LANES, LANES), 1)
    upper_incl = (li <= lj).astype(BF16)
    ci = lax.broadcasted_iota(jnp.int32, (nc, nc), 0)
    cj = lax.broadcasted_iota(jnp.int32, (nc, nc), 1)
    lower_strict = (cj < ci).astype(BF16)
    upper_incl_c = (ci <= cj).astype(BF16)

    p_eq = _dot(eqf.astype(BF16), upper_incl)
    tot_eq = jnp.broadcast_to(p_eq[:, LANES - 1:LANES], (nc, LANES))
    off_eq = _dot(lower_strict, tot_eq.astype(BF16))
    sel = gt | (eq & ((p_eq - eqf + off_eq) < need))
    selb = sel.astype(F32).astype(BF16)

    p_sel = _dot(selb, upper_incl)
    tot_row = _dot_nt(jnp.ones((8, LANES), BF16), selb)
    offi_row = _dot(tot_row.astype(BF16), upper_incl_c)[0:1, :]
    offe_row = offi_row - tot_row[0:1, :]

    slot_c = lax.broadcasted_iota(jnp.int32, (cap, nc), 0).astype(F32)
    chunk_of = jnp.sum((offi_row <= slot_c).astype(F32), axis=1, keepdims=True)
    onehot = lax.broadcasted_iota(jnp.int32, (cap, nc), 1).astype(F32) == chunk_of
    onehot_b = onehot.astype(F32).astype(BF16)
    run = _dot(onehot_b, p_sel.astype(BF16))
    offe = jnp.sum(jnp.where(onehot, offe_row, 0.0), axis=1, keepdims=True)
    slot_l = lax.broadcasted_iota(jnp.int32, (cap, LANES), 0).astype(F32)
    lane_of = jnp.sum(((run + offe) <= slot_l).astype(F32), axis=1, keepdims=True)
    idx_ref[...] = (chunk_of * LANES + lane_of).astype(jnp.int32)

    a1 = a.astype(BF16)
    r1 = a - a1.astype(F32)
    a2 = r1.astype(BF16)
    a3 = (r1 - a2.astype(F32)).astype(BF16)
    rows = _dot(onehot_b, a1) + _dot(onehot_b, a2) + _dot(onehot_b, a3)
    lanes = lax.broadcasted_iota(jnp.int32, (cap, LANES), 1).astype(F32)
    gate_ref[...] = jnp.sum(jnp.where(lanes == lane_of, rows, 0.0), axis=1, keepdims=True)


def _topk(aff_t, cap):
    e, n = aff_t.shape
    nc = n // LANES
    return pl.pallas_call(
        functools.partial(_topk_body, cap=cap),
        grid=(e,),
        in_specs=[pl.BlockSpec((None, nc, LANES), lambda i: (i, 0, 0))],
        out_specs=[
            pl.BlockSpec((None, cap, 1), lambda i: (i, 0, 0)),
            pl.BlockSpec((None, cap, 1), lambda i: (i, 0, 0)),
        ],
        out_shape=[
            jax.ShapeDtypeStruct((e, cap, 1), jnp.int32),
            jax.ShapeDtypeStruct((e, cap, 1), F32),
        ],
        compiler_params=_cparams(("parallel",)),
    )(aff_t.reshape(e, nc, LANES))


def _row_copy(src_hbm, tok, dst_ref, r, sem):
    return pltpu.make_async_copy(src_hbm.at[pl.ds(tok, 1), :], dst_ref.at[pl.ds(r, 1), :], sem)


def _moe_up_body(idx_ref, h_hbm, gate_ref, wg_ref, wu_ref, hid_ref, xe_ref, stage_ref, sem, *, cap, rows,
                 row_chunk):
    e = pl.program_id(0)
    f = pl.program_id(1)

    @pl.when(f == 0)
    def _():
        nchunk = cap // rows

        def start(c, slot):
            def body(r, carry):
                tok = idx_ref[e * cap + c * rows + r]
                _row_copy(h_hbm, tok, stage_ref.at[slot], r, sem.at[slot]).start()
                return carry
            lax.fori_loop(0, rows, body, 0, unroll=DMA_UNROLL)

        def wait(slot):
            pltpu.make_async_copy(h_hbm.at[pl.ds(0, rows), :], stage_ref.at[slot], sem.at[slot]).wait()

        start(0, 0)
        for c in range(nchunk):
            slot = c % 2
            if c + 1 < nchunk:
                start(c + 1, 1 - slot)
            wait(slot)
            xe_ref[c * rows:(c + 1) * rows, :] = stage_ref[slot].astype(BF16)

    wg = wg_ref[...].astype(BF16)
    wu = wu_ref[...].astype(BF16)
    for c in range(cap // row_chunk):
        rs = slice(c * row_chunk, (c + 1) * row_chunk)
        xe = xe_ref[rs, :]
        g = _dot(xe, wg)
        u = _dot(xe, wu)
        hid_ref[rs, :] = (g / (1.0 + jnp.exp(-g)) * u * gate_ref[rs, :]).astype(BF16)


def _moe_up(idx_flat, h2, gate, w_gate, w_up, cap):
    n = h2.shape[0]
    tf = TF_MOE
    rows = min(G_GATHER, cap)
    grid_spec = pltpu.PrefetchScalarGridSpec(
        num_scalar_prefetch=1,
        grid=(N_EXPERTS, EXPERT_FF // tf),
        in_specs=[
            pl.BlockSpec(memory_space=pl.ANY),
            pl.BlockSpec((None, cap, 1), lambda e, f, idx: (e, 0, 0)),
            pl.BlockSpec((None, D_MODEL, tf), lambda e, f, idx: (e, 0, f)),
            pl.BlockSpec((None, D_MODEL, tf), lambda e, f, idx: (e, 0, f)),
        ],
        out_specs=pl.BlockSpec((None, cap, tf), lambda e, f, idx: (e, 0, f)),
        scratch_shapes=[
            pltpu.VMEM((cap, D_MODEL), BF16),
            pltpu.VMEM((2, rows, D_MODEL), F32),
            pltpu.SemaphoreType.DMA((2,)),
        ],
    )
    return pl.pallas_call(
        functools.partial(_moe_up_body, cap=cap, rows=rows, row_chunk=min(ROW_CHUNK_UP, cap)),
        grid_spec=grid_spec,
        out_shape=jax.ShapeDtypeStruct((N_EXPERTS, cap, EXPERT_FF), BF16),
        compiler_params=_cparams(("arbitrary", "arbitrary")),
    )(idx_flat, h2, gate, w_gate, w_up)


def _moe_down_body(idx_ref, hid_ref, wd_ref, x_hbm, out_hbm, acc_ref, res_ref, sem, *, cap, rows, nkt):
    del x_hbm
    e = pl.program_id(0)
    rb = pl.program_id(1)
    kt = pl.program_id(2)
    last_kt = pl.num_programs(2) - 1
    base = e * cap + rb * rows
    gather_sem, scatter_sem = sem.at[0], sem.at[1]

    def wait_scatter():
        pltpu.make_async_copy(res_ref, out_hbm.at[pl.ds(0, rows), :], scatter_sem).wait()

    @pl.when(kt == 0)
    def _():
        @pl.when((e > 0) | (rb > 0))
        def _():
            wait_scatter()

        acc_ref[...] = jnp.zeros(acc_ref.shape, F32)

    seg = rows // nkt
    for u in range(seg):
        r = kt * seg + u
        pltpu.make_async_copy(out_hbm.at[pl.ds(idx_ref[base + r], 1), :], res_ref.at[pl.ds(r, 1), :],
                              gather_sem).start()

    acc_ref[...] += _dot(hid_ref[...], wd_ref[...].astype(BF16))

    @pl.when(kt == last_kt)
    def _():
        pltpu.make_async_copy(out_hbm.at[pl.ds(0, rows), :], res_ref, gather_sem).wait()
        res_ref[...] += acc_ref[...]

        def start(r, carry):
            pltpu.make_async_copy(res_ref.at[pl.ds(r, 1), :], out_hbm.at[pl.ds(idx_ref[base + r], 1), :],
                                  scatter_sem).start()
            return carry
        lax.fori_loop(0, rows, start, 0, unroll=DMA_UNROLL)

        @pl.when((e == pl.num_programs(0) - 1) & (rb == pl.num_programs(1) - 1))
        def _():
            wait_scatter()


def _moe_down(idx_flat, hid, w_down, x2, cap):
    n = x2.shape[0]
    rows = min(R_DOWN, cap)
    tk = TK_DOWN
    grid_spec = pltpu.PrefetchScalarGridSpec(
        num_scalar_prefetch=1,
        grid=(N_EXPERTS, cap // rows, EXPERT_FF // tk),
        in_specs=[
            pl.BlockSpec((None, rows, tk), lambda e, rb, kt, idx: (e, rb, kt)),
            pl.BlockSpec((None, tk, D_MODEL), lambda e, rb, kt, idx: (e, kt, 0)),
            pl.BlockSpec(memory_space=pl.ANY),
        ],
        out_specs=pl.BlockSpec(memory_space=pl.ANY),
        scratch_shapes=[
            pltpu.VMEM((rows, D_MODEL), F32),
            pltpu.VMEM((rows, D_MODEL), F32),
            pltpu.SemaphoreType.DMA((2,)),
        ],
    )
    return pl.pallas_call(
        functools.partial(_moe_down_body, cap=cap, rows=rows, nkt=EXPERT_FF // tk),
        grid_spec=grid_spec,
        out_shape=jax.ShapeDtypeStruct((n, D_MODEL), F32),
        input_output_aliases={3: 0},
        compiler_params=_cparams(("arbitrary", "arbitrary", "arbitrary")),
    )(idx_flat, hid, w_down, x2)


def _pad_rope_cols(t):
    half = QK_ROPE // 2
    z = jnp.zeros(t.shape[:-1] + (half,), t.dtype)
    return jnp.concatenate([t[..., :half], z, t[..., half:], z], axis=-1)


def _rope_tables_padded(seq):
    inv = ROPE_THETA ** (-jnp.arange(0, QK_ROPE, 2, dtype=F32) / QK_ROPE)
    ang = jnp.arange(seq, dtype=F32)[:, None] * inv[None, :]
    cos, sin = jnp.cos(ang), jnp.sin(ang)
    z = jnp.zeros_like(cos)
    return jnp.concatenate([cos, z, cos, z], axis=-1), jnp.concatenate([-sin, z, sin, z], axis=-1)


def _prepare_weights(w_in, mla_w_q_b, mla_w_kv_b, w_o, w_router, mla_q_rope_norm_g, mla_k_rope_norm_g):
    qkv_cols = 3 * A_WIDTH
    w_a = w_in[:, :qkv_cols].astype(BF16)
    w_ql = w_in[:, qkv_cols:qkv_cols + Q_LORA].astype(BF16)
    w_kvl = w_in[:, qkv_cols + Q_LORA:qkv_cols + Q_LORA + KV_LORA].astype(BF16)
    w_kr = _pad_rope_cols(w_in[:, qkv_cols + Q_LORA + KV_LORA:]).astype(BF16)
    wq = mla_w_q_b.reshape(Q_LORA, MLA_HEADS, QK_NOPE + QK_ROPE)
    wq_pad = jnp.concatenate([wq[..., :QK_NOPE], _pad_rope_cols(wq[..., QK_NOPE:])], axis=-1)
    wq_pad = wq_pad.reshape(Q_LORA, MLA_HEADS * MLA_HEAD_PAD).astype(BF16)
    wr = jnp.pad(w_router, ((0, 0), (0, LANES - N_EXPERTS)))
    wr1 = wr.astype(BF16)
    wr2 = (wr - wr1.astype(F32)).astype(BF16)
    return dict(
        w_a=w_a, w_ql=w_ql, w_kvl=w_kvl, w_kr=w_kr, wq_pad=wq_pad,
        wkv=mla_w_kv_b.astype(BF16), w_o=w_o.astype(BF16), wr1=wr1, wr2=wr2,
        gqr_pad=_pad_rope_cols(mla_q_rope_norm_g)[None, :], gkr_pad=_pad_rope_cols(mla_k_rope_norm_g)[None, :],
    )


def _encoder_layer(x3, pw, norm_mix_g, a_q_norm_g, a_k_norm_g, mla_q_a_norm_g, mla_kv_a_norm_g,
                   mla_q_nope_norm_g, mla_k_nope_norm_g, out_norm_a_g, out_norm_b_g, norm_ffn_g,
                   w_gate, w_up, w_down):
    b, s, d = x3.shape
    n = b * s
    x = x3.reshape(n, d)
    row = lambda v: v[None, :]
    cos_pad, sin_pad = _rope_tables_padded(s)

    qkv_by_dilation = _inproj_a(x, row(norm_mix_g), pw["w_a"], row(a_q_norm_g), row(a_k_norm_g), b, s)
    branches = [_dilated_branch(qkv) for qkv in qkv_by_dilation]

    cq, ckv, kr = _inproj_lat(x, row(norm_mix_g), pw["w_ql"], pw["w_kvl"], pw["w_kr"], row(mla_q_a_norm_g),
                              row(mla_kv_a_norm_g), pw["gkr_pad"], cos_pad, sin_pad, s)
    qm, km, vm = _mla_up(cq, ckv, kr, pw["wq_pad"], pw["wkv"], row(mla_q_nope_norm_g), pw["gqr_pad"],
                         row(mla_k_nope_norm_g), cos_pad, sin_pad, s)
    wide = MLA_HEADS * MLA_HEAD_PAD
    ob = _mla_flash(qm.reshape(b, s, wide), km.reshape(b, s, wide), vm.reshape(b, s, wide))

    x2 = _outproj(branches, ob.reshape(n, MLA_WIDTH), row(out_norm_a_g), row(out_norm_b_g), pw["w_o"], x, s)

    h2, aff_t = _router(x2, row(norm_ffn_g), pw["wr1"], pw["wr2"])
    cap = CAPACITY_FACTOR * n // N_EXPERTS
    idx, gate = _topk(aff_t, cap)
    idx_flat = idx.reshape(N_EXPERTS * cap)
    hid = _moe_up(idx_flat, h2, gate, w_gate, w_up, cap)
    y = _moe_down(idx_flat, hid, w_down, x2, cap)
    return y.reshape(b, s, d)


def kernel(x_prompt, x_sample, norm_mix_g, w_in, a_q_norm_g, a_k_norm_g, mla_q_a_norm_g, mla_w_q_b, mla_kv_a_norm_g, mla_w_kv_b, mla_q_nope_norm_g, mla_q_rope_norm_g, mla_k_nope_norm_g, mla_k_rope_norm_g, out_norm_a_g, out_norm_b_g, w_o, norm_ffn_g, w_router, w_gate, w_up, w_down):
    depth = w_in.shape[0]
    prepared = [_prepare_weights(w_in[l], mla_w_q_b[l], mla_w_kv_b[l], w_o[l], w_router[l],
                                 mla_q_rope_norm_g[l], mla_k_rope_norm_g[l]) for l in range(depth)]

    def run_trunk(x):
        for l in range(depth):
            pw = prepared[l]
            x = _encoder_layer(x, pw, norm_mix_g[l], a_q_norm_g[l], a_k_norm_g[l], mla_q_a_norm_g[l],
                               mla_kv_a_norm_g[l], mla_q_nope_norm_g[l], mla_k_nope_norm_g[l],
                               out_norm_a_g[l], out_norm_b_g[l], norm_ffn_g[l], w_gate[l], w_up[l], w_down[l])
        return x

    return run_trunk(x_prompt), run_trunk(x_sample)
```

```python
import functools
import math

import jax
import jax.numpy as jnp
from jax import lax
from jax.experimental import pallas as pl
from jax.experimental.pallas import tpu as pltpu

F32 = jnp.float32
BF16 = jnp.bfloat16

D_MODEL = 4096
A_HEADS = 16
A_HEAD_DIM = 128
A_WIDTH = A_HEADS * A_HEAD_DIM
DILATED_PATTERNS = ((128, 1), (512, 4), (2048, 16))
DILATIONS = tuple(d for _, d in DILATED_PATTERNS)
MLA_HEADS = 16
QK_NOPE = 128
QK_ROPE = 64
V_HEAD = 128
Q_LORA = 1024
KV_LORA = 512
MLA_WIDTH = MLA_HEADS * V_HEAD
ROPE_THETA = 10000.0
N_EXPERTS = 16
EXPERT_FF = 2048
CAPACITY_FACTOR = 2
EPS = 1e-6

LANES = 128
MLA_HEAD_PAD = 2 * LANES
N_SIDE = 64

TM_INPROJ = 512
TN_INPROJ = 512
TM_LAT = 256
TM_MLAUP = 256
TQ_MLA = 1024
TK_MLA = 2048
TQ_DIL = 128
TB_DIL = 512
TM_OUT = 512
TN_OUT = 512
TM_ROUTER = 256
TF_MOE = 256
ROW_CHUNK_UP = 512
TK_DOWN = 256
R_DOWN = 1024
G_GATHER = 256
DMA_UNROLL = 8
VMEM_LIMIT = 56 * 1024 * 1024


def _cparams(sem):
    return pltpu.CompilerParams(dimension_semantics=sem, vmem_limit_bytes=VMEM_LIMIT)


def _rms(x, g):
    ms = jnp.mean(x * x, axis=-1, keepdims=True)
    return x * lax.rsqrt(ms + EPS) * g


def _dot(a, b):
    return jnp.dot(a, b, preferred_element_type=F32)


def _dot_nt(a, b):
    return lax.dot_general(a, b, (((1,), (1,)), ((), ())), preferred_element_type=F32)


def _inproj_a_body(x_ref, g_ref, w_ref, gq_ref, gk_ref, *refs, chunk, ncol_tiles):
    nd = len(DILATIONS)
    out_refs = refs[:nd]
    xn_ref, y0_ref, y1_ref, z_ref = refs[nd:]
    t = pl.program_id(0)
    j = t % ncol_tiles
    jp = (t + ncol_tiles - 1) % ncol_tiles
    tm = x_ref.shape[0]
    tn = w_ref.shape[1]
    nq = A_WIDTH // tn
    planes = tn // LANES

    @pl.when(t == 0)
    def _():
        y1_ref[...] = jnp.zeros(y1_ref.shape, F32)

    @pl.when(j == 0)
    def _():
        def norm_chunk(c, carry):
            r = pl.multiple_of(c * chunk, chunk)
            xn_ref[pl.ds(r, chunk), :] = _rms(x_ref[pl.ds(r, chunk), :], g_ref[...]).astype(BF16)
            return carry

        lax.fori_loop(0, tm // chunk, norm_chunk, 0)

    is_q = jp < nq
    is_k = (jp >= nq) & (jp < 2 * nq)
    q_gain = gq_ref[...] * (A_HEAD_DIM ** -0.5 * math.log2(math.e))
    gain = jnp.where(is_q, q_gain, jnp.where(is_k, gk_ref[...], jnp.ones_like(q_gain)))

    def work(y_cur, y_prev):
        y_cur[...] = _dot(xn_ref[...], w_ref[...])
        for c in range(planes):
            yh = y_prev[:, c * LANES:(c + 1) * LANES]
            ms = jnp.mean(yh * yh, axis=-1, keepdims=True)
            factor = jnp.where(is_q | is_k, lax.rsqrt(ms + EPS), jnp.ones_like(ms))
            z_ref[c] = yh * factor * gain
        for o_ref, d in zip(out_refs, DILATIONS):
            for c in range(planes):
                cols = slice(c * LANES, (c + 1) * LANES)
                if d == 1:
                    o_ref[:, cols] = z_ref[c].astype(BF16)
                else:
                    for r in range(d):
                        o_ref[r, :, cols] = z_ref.at[c][pl.ds(r, tm // d, stride=d), :].astype(BF16)

    @pl.when(t % 2 == 0)
    def _():
        work(y0_ref, y1_ref)

    @pl.when(t % 2 == 1)
    def _():
        work(y1_ref, y0_ref)


def _inproj_a(x, g, w_a, gq, gk, batch, seq):
    n = x.shape[0]
    tm = min(TM_INPROJ, seq)
    tn = TN_INPROJ
    ncol = w_a.shape[1]
    nseq = seq // tm
    nrow_tiles = n // tm
    nct = ncol // tn
    steps = nrow_tiles * nct + 1

    def prev_tile(t):
        tp = jnp.maximum(t - 1, 0)
        return tp // nct, tp % nct

    def residue_major(t):
        i, j = prev_tile(t)
        return i // nseq, 0, i % nseq, j

    out_specs, out_shape = [], []
    for d in DILATIONS:
        if d == 1:
            out_specs.append(pl.BlockSpec((tm, tn), prev_tile))
            out_shape.append(jax.ShapeDtypeStruct((n, ncol), BF16))
        else:
            out_specs.append(pl.BlockSpec((None, d, tm // d, tn), residue_major))
            out_shape.append(jax.ShapeDtypeStruct((batch, d, seq // d, ncol), BF16))
    outs = pl.pallas_call(
        functools.partial(_inproj_a_body, chunk=128, ncol_tiles=nct),
        grid=(steps,),
        in_specs=[
            pl.BlockSpec((tm, D_MODEL), lambda t: (jnp.minimum(t // nct, nrow_tiles - 1), 0)),
            pl.BlockSpec((1, D_MODEL), lambda t: (0, 0)),
            pl.BlockSpec((D_MODEL, tn), lambda t: (0, t % nct)),
            pl.BlockSpec((1, A_HEAD_DIM), lambda t: (0, 0)),
            pl.BlockSpec((1, A_HEAD_DIM), lambda t: (0, 0)),
        ],
        out_specs=out_specs,
        out_shape=out_shape,
        scratch_shapes=[
            pltpu.VMEM((tm, D_MODEL), BF16),
            pltpu.VMEM((tm, tn), F32),
            pltpu.VMEM((tm, tn), F32),
            pltpu.VMEM((tn // LANES, tm, LANES), F32),
        ],
        compiler_params=_cparams(("arbitrary",)),
    )(x, g, w_a, gq, gk)
    return [o.reshape(batch, d, seq // d, ncol) for o, d in zip(outs, DILATIONS)]


def _rope_padded(t, cos, sin):
    return t * cos + pltpu.roll(t, 2 * (QK_ROPE // 2), 1) * sin


def _inproj_lat_body(x_ref, g_ref, wq_ref, wkv_ref, wkr_ref, gq_ref, gkv_ref, gkr_ref, cos_ref, sin_ref,
                     cq_ref, ckv_ref, kr_ref):
    xn = _rms(x_ref[...], g_ref[...]).astype(BF16)
    cq_ref[...] = _rms(_dot(xn, wq_ref[...]), gq_ref[...]).astype(BF16)
    ckv_ref[...] = _rms(_dot(xn, wkv_ref[...]), gkv_ref[...]).astype(BF16)
    kr = _dot(xn, wkr_ref[...])
    ms = jnp.sum(kr * kr, axis=-1, keepdims=True) * (1.0 / QK_ROPE)
    krn = kr * lax.rsqrt(ms + EPS) * gkr_ref[...]
    kr_ref[...] = _rope_padded(krn, cos_ref[...], sin_ref[...]).astype(BF16)


def _inproj_lat(x, g, wq, wkv, wkr, gq, gkv, gkr, cos_pad, sin_pad, seq):
    n = x.shape[0]
    tm = min(TM_LAT, seq)
    nseq = seq // tm
    const = lambda i: (0, 0)
    return pl.pallas_call(
        _inproj_lat_body,
        grid=(n // tm,),
        in_specs=[
            pl.BlockSpec((tm, D_MODEL), lambda i: (i, 0)),
            pl.BlockSpec((1, D_MODEL), const),
            pl.BlockSpec((D_MODEL, Q_LORA), const),
            pl.BlockSpec((D_MODEL, KV_LORA), const),
            pl.BlockSpec((D_MODEL, LANES), const),
            pl.BlockSpec((1, Q_LORA), const),
            pl.BlockSpec((1, KV_LORA), const),
            pl.BlockSpec((1, LANES), const),
            pl.BlockSpec((tm, LANES), lambda i: (i % nseq, 0)),
            pl.BlockSpec((tm, LANES), lambda i: (i % nseq, 0)),
        ],
        out_specs=[
            pl.BlockSpec((tm, Q_LORA), lambda i: (i, 0)),
            pl.BlockSpec((tm, KV_LORA), lambda i: (i, 0)),
            pl.BlockSpec((tm, LANES), lambda i: (i, 0)),
        ],
        out_shape=[
            jax.ShapeDtypeStruct((n, Q_LORA), BF16),
            jax.ShapeDtypeStruct((n, KV_LORA), BF16),
            jax.ShapeDtypeStruct((n, LANES), BF16),
        ],
        compiler_params=_cparams(("parallel",)),
    )(x, g, wq, wkv, wkr, gq, gkv, gkr, cos_pad, sin_pad)


def _mla_up_body(cq_ref, ckv_ref, kr_ref, wq_ref, wkv_ref, gqn_ref, gqr_ref, gkn_ref, cos_ref, sin_ref,
                 q_ref, k_ref, v_ref):
    cq = cq_ref[...]
    ckv = ckv_ref[...]
    kr = kr_ref[...]
    cos = cos_ref[...]
    sin = sin_ref[...]
    scale = (QK_NOPE + QK_ROPE) ** -0.5 * math.log2(math.e)
    ones = jnp.ones((cq.shape[0], V_HEAD), BF16)
    for h in range(MLA_HEADS):
        lo = h * MLA_HEAD_PAD
        q = _dot(cq, wq_ref[:, lo:lo + MLA_HEAD_PAD])
        qn = _rms(q[:, :LANES], gqn_ref[...] * scale)
        qr = q[:, LANES:]
        ms = jnp.sum(qr * qr, axis=-1, keepdims=True) * (1.0 / QK_ROPE)
        qr = _rope_padded(qr * lax.rsqrt(ms + EPS) * (gqr_ref[...] * scale), cos, sin)
        q_ref[:, lo:lo + LANES] = qn.astype(BF16)
        q_ref[:, lo + LANES:lo + MLA_HEAD_PAD] = qr.astype(BF16)
        kv = _dot(ckv, wkv_ref[:, lo:lo + MLA_HEAD_PAD])
        k_ref[:, lo:lo + LANES] = _rms(kv[:, :LANES], gkn_ref[...]).astype(BF16)
        k_ref[:, lo + LANES:lo + MLA_HEAD_PAD] = kr
        v_ref[:, lo:lo + V_HEAD] = kv[:, LANES:].astype(BF16)
        v_ref[:, lo + V_HEAD:lo + 2 * V_HEAD] = ones


def _mla_up(cq, ckv, kr, wq_pad, wkv, gqn, gqr_pad, gkn, cos_pad, sin_pad, seq):
    n = cq.shape[0]
    tm = min(TM_MLAUP, seq)
    nseq = seq // tm
    const = lambda i: (0, 0)
    row = lambda i: (i, 0)
    wide = MLA_HEADS * MLA_HEAD_PAD
    return pl.pallas_call(
        _mla_up_body,
        grid=(n // tm,),
        in_specs=[
            pl.BlockSpec((tm, Q_LORA), row),
            pl.BlockSpec((tm, KV_LORA), row),
            pl.BlockSpec((tm, LANES), row),
            pl.BlockSpec((Q_LORA, wide), const),
            pl.BlockSpec((KV_LORA, wide), const),
            pl.BlockSpec((1, LANES), const),
            pl.BlockSpec((1, LANES), const),
            pl.BlockSpec((1, LANES), const),
            pl.BlockSpec((tm, LANES), lambda i: (i % nseq, 0)),
            pl.BlockSpec((tm, LANES), lambda i: (i % nseq, 0)),
        ],
        out_specs=[
            pl.BlockSpec((tm, wide), row),
            pl.BlockSpec((tm, wide), row),
            pl.BlockSpec((tm, wide), row),
        ],
        out_shape=[
            jax.ShapeDtypeStruct((n, wide), BF16),
            jax.ShapeDtypeStruct((n, wide), BF16),
            jax.ShapeDtypeStruct((n, wide), BF16),
        ],
        compiler_params=_cparams(("parallel",)),
    )(cq, ckv, kr, wq_pad, wkv, gqn, gqr_pad, gkn, cos_pad, sin_pad)


def _mla_flash_body(q_ref, k_ref, v_ref, o_ref, m_ref, acc_ref, s0_ref, s1_ref, *, tk):
    nk = k_ref.shape[0] // tk
    m_ref[...] = jnp.full(m_ref.shape, -jnp.inf, F32)
    acc_ref[...] = jnp.zeros(acc_ref.shape, F32)

    def scores(c, s_ref):
        r = pl.multiple_of(c * tk, tk)
        s_ref[...] = _dot_nt(q_ref[...], k_ref[pl.ds(r, tk), :])

    def update(c, s_ref):
        r = pl.multiple_of(c * tk, tk)
        s = s_ref[...]
        m_old = m_ref[...]
        m_new = jnp.maximum(m_old, jnp.max(s, axis=-1, keepdims=True))
        alpha = jnp.exp2(m_old - m_new)
        p = jnp.exp2((s - jnp.tile(m_new, (1, tk // LANES))).astype(BF16))
        acc_ref[...] = jnp.tile(alpha, (1, 2)) * acc_ref[...] + _dot(p, v_ref[pl.ds(r, tk), :])
        m_ref[...] = m_new

    scores(0, s0_ref)

    def pair(jj, carry):
        c = 2 * jj
        scores(c + 1, s1_ref)
        update(c, s0_ref)
        scores(c + 2, s0_ref)
        update(c + 1, s1_ref)
        return carry

    lax.fori_loop(0, nk // 2 - 1, pair, 0)
    scores(nk - 1, s1_ref)
    update(nk - 2, s0_ref)
    update(nk - 1, s1_ref)
    o_ref[...] = (acc_ref[:, :V_HEAD] / acc_ref[:, V_HEAD:]).astype(o_ref.dtype)


def _mla_flash(qm, km, vm):
    b, s, _ = qm.shape
    tq = min(TQ_MLA, s)
    tk = min(TK_MLA, s // 2)
    return pl.pallas_call(
        functools.partial(_mla_flash_body, tk=tk),
        grid=(b, MLA_HEADS, s // tq),
        in_specs=[
            pl.BlockSpec((None, tq, MLA_HEAD_PAD), lambda bi, h, i: (bi, i, h)),
            pl.BlockSpec((None, s, MLA_HEAD_PAD), lambda bi, h, i: (bi, 0, h)),
            pl.BlockSpec((None, s, 2 * V_HEAD), lambda bi, h, i: (bi, 0, h)),
        ],
        out_specs=pl.BlockSpec((None, tq, V_HEAD), lambda bi, h, i: (bi, i, h)),
        out_shape=jax.ShapeDtypeStruct((b, s, MLA_WIDTH), BF16),
        scratch_shapes=[pltpu.VMEM((tq, LANES), F32), pltpu.VMEM((tq, 2 * V_HEAD), F32),
                        pltpu.VMEM((tq, tk), F32), pltpu.VMEM((tq, tk), F32)],
        compiler_params=_cparams(("parallel", "parallel", "arbitrary")),
    )(qm, km, vm)


def _dilated_body(q_ref, k0_ref, kp_ref, kn_ref, v0_ref, vp_ref, vn_ref, o_ref, lse_ref, kw_ref, vw_ref,
                  s_ref, *, dil, nq, uq):
    i = pl.program_id(2)
    tb = q_ref.shape[0]
    units = tb // uq
    kw_ref[0:N_SIDE, :] = kp_ref[...]
    kw_ref[N_SIDE:N_SIDE + tb, :] = k0_ref[...]
    kw_ref[N_SIDE + tb:, :] = kn_ref[...]
    vw_ref[0:N_SIDE, :] = vp_ref[...]
    vw_ref[N_SIDE:N_SIDE + tb, :] = v0_ref[...]
    vw_ref[N_SIDE + tb:, :] = vn_ref[...]

    @pl.when(i == nq - 1)
    def _():
        kw_ref[N_SIDE + tb:, :] = jnp.zeros((N_SIDE, A_WIDTH), BF16)
        vw_ref[N_SIDE + tb:, :] = jnp.zeros((N_SIDE, A_WIDTH), BF16)

    win = uq + 2 * N_SIDE
    row = lax.broadcasted_iota(jnp.int32, (uq, win), 0)
    col = lax.broadcasted_iota(jnp.int32, (uq, win), 1)
    rel = col - N_SIDE - row
    arel = jnp.abs(rel)
    near = arel <= N_SIDE
    token_dist = (dil * arel).astype(F32)
    lane = lax.broadcasted_iota(jnp.int32, (uq, LANES), 1)
    ones = jnp.ones((win, LANES), BF16)

    for u in range(units):
        valid = near
        if u == 0:
            valid = valid & ((rel >= 0) | (row >= N_SIDE) | (i > 0))
        if u == units - 1:
            valid = valid & ((rel < N_SIDE) | (row < uq - N_SIDE) | (i < nq - 1))
        dist = jnp.where(valid, token_dist, jnp.inf)
        qrows = slice(u * uq, (u + 1) * uq)
        wrows = slice(u * uq, u * uq + win)
        sbuf = s_ref.at[u % 2]
        for h in range(A_HEADS):
            sl = slice(h * A_HEAD_DIM, (h + 1) * A_HEAD_DIM)
            sbuf[h] = _dot_nt(q_ref[qrows, sl], kw_ref[wrows, sl])
        lse_all = jnp.zeros((uq, LANES), F32)
        for h in range(A_HEADS):
            sl = slice(h * A_HEAD_DIM, (h + 1) * A_HEAD_DIM)
            slope = 2.0 ** (-8.0 * (h + 1) / A_HEADS) * math.log2(math.e)
            s = sbuf[h] - slope * dist
            m = jnp.max(s, axis=-1, keepdims=True)
            p = jnp.exp2((s - m).astype(BF16))
            l = _dot(p, ones)
            o_ref[qrows, sl] = (_dot(p, vw_ref[wrows, sl]) / l).astype(o_ref.dtype)
            lse_all = jnp.where(lane == h, m + jnp.log2(l), lse_all)
        lse_ref[qrows, :] = lse_all


def _dilated_branch(qkv):
    b, dil, sub, _ = qkv.shape
    uq = TQ_DIL
    tq = min(TB_DIL, sub)
    nq = sub // tq
    nhalo = sub // N_SIDE
    per = tq // N_SIDE

    def own(which):
        return pl.BlockSpec((None, None, tq, A_WIDTH), lambda bi, r, i: (bi, r, i, which))

    def prev(which):
        return pl.BlockSpec((None, None, N_SIDE, A_WIDTH),
                            lambda bi, r, i: (bi, r, jnp.maximum(per * i - 1, 0), which))

    def nxt(which):
        return pl.BlockSpec((None, None, N_SIDE, A_WIDTH),
                            lambda bi, r, i: (bi, r, jnp.minimum(per * (i + 1), nhalo - 1), which))

    return pl.pallas_call(
        functools.partial(_dilated_body, dil=dil, nq=nq, uq=uq),
        grid=(b, dil, nq),
        in_specs=[own(0), own(1), prev(1), nxt(1), own(2), prev(2), nxt(2)],
        out_specs=[
            pl.BlockSpec((None, None, tq, A_WIDTH), lambda bi, r, i: (bi, r, i, 0)),
            pl.BlockSpec((None, None, tq, LANES), lambda bi, r, i: (bi, r, i, 0)),
        ],
        out_shape=[
            jax.ShapeDtypeStruct((b, dil, sub, A_WIDTH), BF16),
            jax.ShapeDtypeStruct((b, dil, sub, LANES), F32),
        ],
        scratch_shapes=[
            pltpu.VMEM((tq + 2 * N_SIDE, A_WIDTH), BF16),
            pltpu.VMEM((tq + 2 * N_SIDE, A_WIDTH), BF16),
            pltpu.VMEM((2, A_HEADS, uq, uq + 2 * N_SIDE), F32),
        ],
        compiler_params=_cparams(("parallel", "parallel", "arbitrary")),
    )(qkv, qkv, qkv, qkv, qkv, qkv, qkv)


def _outproj_body(*refs):
    nd = len(DILATIONS)
    o_refs, l_refs = refs[:nd], refs[nd:2 * nd]
    ob_ref, ga_ref, gb_ref, w_ref, x_ref, y_ref, mix_ref, oa_ref, t_ref, lt_ref = refs[2 * nd:]
    j = pl.program_id(1)
    tm = x_ref.shape[0]

    def rows_of(d, r):
        return slice(None) if d == 1 else pl.ds(r, tm // d, stride=d)

    @pl.when(j == 0)
    def _():
        for i, d in enumerate(DILATIONS):
            for r in range(d):
                lt_ref.at[i][rows_of(d, r), :] = l_refs[i][r]
        ls = [lt_ref[i] for i in range(nd)]
        mx = functools.reduce(jnp.maximum, ls)
        es = [jnp.exp2(l - mx) for l in ls]
        inv = 1.0 / functools.reduce(lambda a, b: a + b, es)
        head_of_col = lax.broadcasted_iota(jnp.int32, (2 * LANES, A_WIDTH), 1) // A_HEAD_DIM
        lane_of_row = lax.broadcasted_iota(jnp.int32, (2 * LANES, A_WIDTH), 0) % LANES
        expand = (head_of_col == lane_of_row).astype(BF16)
        for i, d in enumerate(DILATIONS):
            w = es[i] * inv
            hi = w.astype(BF16)
            lo = (w - hi.astype(F32)).astype(BF16)
            wexp = _dot(jnp.concatenate([hi, lo], axis=1), expand)
            for h in range(A_HEADS):
                sl = slice(h * A_HEAD_DIM, (h + 1) * A_HEAD_DIM)
                for r in range(d):
                    t_ref.at[h][rows_of(d, r), :] = o_refs[i][r, :, sl].astype(F32)
                if i == 0:
                    oa_ref[h] = wexp[:, sl] * t_ref[h]
                else:
                    oa_ref[h] += wexp[:, sl] * t_ref[h]
        ss = jnp.zeros((tm, 1), F32)
        for h in range(A_HEADS):
            oa = oa_ref[h]
            ss = ss + jnp.sum(oa * oa, axis=-1, keepdims=True)
        inv_a = lax.rsqrt(ss * (1.0 / A_WIDTH) + EPS)
        for h in range(A_HEADS):
            sl = slice(h * A_HEAD_DIM, (h + 1) * A_HEAD_DIM)
            mix_ref[:, sl] = (oa_ref[h] * inv_a * ga_ref[:, sl]).astype(BF16)
        mix_ref[:, A_WIDTH:] = _rms(ob_ref[...].astype(F32), gb_ref[...]).astype(BF16)

    y_ref[...] = x_ref[...] + _dot(mix_ref[...], w_ref[...])


def _outproj(branches, ob, ga, gb, w_o, x, seq):
    n = x.shape[0]
    tm = min(TM_OUT, seq)
    tn = TN_OUT
    nseq = seq // tm
    row = lambda i, j: (i, 0)
    const = lambda i, j: (0, 0)
    mixw = A_WIDTH + MLA_WIDTH
    res = lambda i, j: (i // nseq, 0, i % nseq, 0)
    o_specs = [pl.BlockSpec((None, d, tm // d, A_WIDTH), res) for d in DILATIONS]
    l_specs = [pl.BlockSpec((None, d, tm // d, LANES), res) for d in DILATIONS]
    return pl.pallas_call(
        _outproj_body,
        grid=(n // tm, D_MODEL // tn),
        in_specs=o_specs + l_specs + [
            pl.BlockSpec((tm, MLA_WIDTH), row),
            pl.BlockSpec((1, A_WIDTH), const), pl.BlockSpec((1, MLA_WIDTH), const),
            pl.BlockSpec((mixw, tn), lambda i, j: (0, j)),
            pl.BlockSpec((tm, tn), lambda i, j: (i, j)),
        ],
        out_specs=pl.BlockSpec((tm, tn), lambda i, j: (i, j)),
        out_shape=jax.ShapeDtypeStruct((n, D_MODEL), F32),
        scratch_shapes=[
            pltpu.VMEM((tm, mixw), BF16),
            pltpu.VMEM((A_HEADS, tm, A_HEAD_DIM), F32),
            pltpu.VMEM((A_HEADS, tm, A_HEAD_DIM), F32),
            pltpu.VMEM((len(DILATIONS), tm, LANES), F32),
        ],
        compiler_params=_cparams(("parallel", "arbitrary")),
    )(*[o for o, _ in branches], *[l for _, l in branches], ob, ga, gb, w_o, x)


def _router_body(x_ref, g_ref, w1_ref, w2_ref, h_ref, aff_ref):
    h = _rms(x_ref[...], g_ref[...])
    h_ref[...] = h
    h1 = h.astype(BF16)
    h2 = (h - h1.astype(F32)).astype(BF16)
    logits = _dot(h1, w1_ref[...]) + _dot(h1, w2_ref[...]) + _dot(h2, w1_ref[...])
    lane = lax.broadcasted_iota(jnp.int32, logits.shape, 1)
    logits = jnp.where(lane < N_EXPERTS, logits, -jnp.inf)
    m = jnp.max(logits, axis=-1, keepdims=True)
    e = jnp.exp(logits - m)
    aff_ref[...] = e / jnp.sum(e, axis=-1, keepdims=True)


def _router(x2, g, wr1, wr2):
    n = x2.shape[0]
    tm = min(TM_ROUTER, n)
    h2, aff = pl.pallas_call(
        _router_body,
        grid=(n // tm,),
        in_specs=[
            pl.BlockSpec((tm, D_MODEL), lambda i: (i, 0)),
            pl.BlockSpec((1, D_MODEL), lambda i: (0, 0)),
            pl.BlockSpec((D_MODEL, LANES), lambda i: (0, 0)),
            pl.BlockSpec((D_MODEL, LANES), lambda i: (0, 0)),
        ],
        out_specs=[
            pl.BlockSpec((tm, D_MODEL), lambda i: (i, 0)),
            pl.BlockSpec((tm, LANES), lambda i: (i, 0)),
        ],
        out_shape=[
            jax.ShapeDtypeStruct((n, D_MODEL), F32),
            jax.ShapeDtypeStruct((n, LANES), F32),
        ],
        compiler_params=_cparams(("parallel",)),
    )(x2, g, wr1, wr2)
    return h2, aff[:, :N_EXPERTS].T


def _sum_all(x):
    return jnp.sum(jnp.sum(x, axis=1, keepdims=True), axis=0, keepdims=True)


def _topk_body(a_ref, idx_ref, gate_ref, *, cap):
    a = a_ref[...]
    nc = a.shape[0]
    bits = pltpu.bitcast(a, jnp.int32)
    thr = jnp.zeros((1, 1), jnp.int32)
    for bit in range(30, -1, -1):
        cand = thr + (1 << bit)
        cnt = _sum_all((bits >= cand).astype(F32))
        thr = jnp.where(cnt >= cap, cand, thr)
    gt = bits > thr
    eq = bits == thr
    eqf = eq.astype(F32)
    need = cap - _sum_all(gt.astype(F32))

    li = lax.broadcasted_iota(jnp.int32, (LANES, LANES), 0)
    lj = lax.broadcasted_iota(jnp.int32, (LANES, LANES), 1)
    upper_incl = (li <= lj).astype(BF16)
    ci = lax.broadcasted_iota(jnp.int32, (nc, nc), 0)
    cj = lax.broadcasted_iota(jnp.int32, (nc, nc), 1)
    lower_strict = (cj < ci).astype(BF16)
    upper_incl_c = (ci <= cj).astype(BF16)

    p_eq = _dot(eqf.astype(BF16), upper_incl)
    tot_eq = jnp.broadcast_to(p_eq[:, LANES - 1:LANES], (nc, LANES))
    off_eq = _dot(lower_strict, tot_eq.astype(BF16))
    sel = gt | (eq & ((p_eq - eqf + off_eq) < need))
    selb = sel.astype(F32).astype(BF16)

    p_sel = _dot(selb, upper_incl)
    tot_row = _dot_nt(jnp.ones((8, LANES), BF16), selb)
    offi_row = _dot(tot_row.astype(BF16), upper_incl_c)[0:1, :]
    offe_row = offi_row - tot_row[0:1, :]

    slot_c = lax.broadcasted_iota(jnp.int32, (cap, nc), 0).astype(F32)
    chunk_of = jnp.sum((offi_row <= slot_c).astype(F32), axis=1, keepdims=True)
    onehot = lax.broadcasted_iota(jnp.int32, (cap, nc), 1).astype(F32) == chunk_of
    onehot_b = onehot.astype(F32).astype(BF16)
    run = _dot(onehot_b, p_sel.astype(BF16))
    offe = jnp.sum(jnp.where(onehot, offe_row, 0.0), axis=1, keepdims=True)
    slot_l = lax.broadcasted_iota(jnp.int32, (cap, LANES), 0).astype(F32)
    lane_of = jnp.sum(((run + offe) <= slot_l).astype(F32), axis=1, keepdims=True)
    idx_ref[...] = (chunk_of * LANES + lane_of).astype(jnp.int32)

    a1 = a.astype(BF16)
    r1 = a - a1.astype(F32)
    a2 = r1.astype(BF16)
    a3 = (r1 - a2.astype(F32)).astype(BF16)
    rows = _dot(onehot_b, a1) + _dot(onehot_b, a2) + _dot(onehot_b, a3)
    lanes = lax.broadcasted_iota(jnp.int32, (cap, LANES), 1).astype(F32)
    gate_ref[...] = jnp.sum(jnp.where(lanes == lane_of, rows, 0.0), axis=1, keepdims=True)


def _topk(aff_t, cap):
    e, n = aff_t.shape
    nc = n // LANES
    return pl.pallas_call(
        functools.partial(_topk_body, cap=cap),
        grid=(e,),
        in_specs=[pl.BlockSpec((None, nc, LANES), lambda i: (i, 0, 0))],
        out_specs=[
            pl.BlockSpec((None, cap, 1), lambda i: (i, 0, 0)),
            pl.BlockSpec((None, cap, 1), lambda i: (i, 0, 0)),
        ],
        out_shape=[
            jax.ShapeDtypeStruct((e, cap, 1), jnp.int32),
            jax.ShapeDtypeStruct((e, cap, 1), F32),
        ],
        compiler_params=_cparams(("parallel",)),
    )(aff_t.reshape(e, nc, LANES))


def _row_copy(src_hbm, tok, dst_ref, r, sem):
    return pltpu.make_async_copy(src_hbm.at[pl.ds(tok, 1), :], dst_ref.at[pl.ds(r, 1), :], sem)


def _moe_up_body(idx_ref, h_hbm, gate_ref, wg_ref, wu_ref, hid_ref, xe_ref, stage_ref, sem, *, cap, rows,
                 row_chunk):
    e = pl.program_id(0)
    f = pl.program_id(1)

    @pl.when(f == 0)
    def _():
        nchunk = cap // rows

        def start(c, slot):
            def body(r, carry):
                tok = idx_ref[e * cap + c * rows + r]
                _row_copy(h_hbm, tok, stage_ref.at[slot], r, sem.at[slot]).start()
                return carry
            lax.fori_loop(0, rows, body, 0, unroll=DMA_UNROLL)

        def wait(slot):
            pltpu.make_async_copy(h_hbm.at[pl.ds(0, rows), :], stage_ref.at[slot], sem.at[slot]).wait()

        start(0, 0)
        for c in range(nchunk):
            slot = c % 2
            if c + 1 < nchunk:
                start(c + 1, 1 - slot)
            wait(slot)
            xe_ref[c * rows:(c + 1) * rows, :] = stage_ref[slot].astype(BF16)

    wg = wg_ref[...].astype(BF16)
    wu = wu_ref[...].astype(BF16)
    for c in range(cap // row_chunk):
        rs = slice(c * row_chunk, (c + 1) * row_chunk)
        xe = xe_ref[rs, :]
        g = _dot(xe, wg)
        u = _dot(xe, wu)
        hid_ref[rs, :] = (g / (1.0 + jnp.exp(-g)) * u * gate_ref[rs, :]).astype(BF16)


def _moe_up(idx_flat, h2, gate, w_gate, w_up, cap):
    n = h2.shape[0]
    tf = TF_MOE
    rows = min(G_GATHER, cap)
    grid_spec = pltpu.PrefetchScalarGridSpec(
        num_scalar_prefetch=1,
        grid=(N_EXPERTS, EXPERT_FF // tf),
        in_specs=[
            pl.BlockSpec(memory_space=pl.ANY),
            pl.BlockSpec((None, cap, 1), lambda e, f, idx: (e, 0, 0)),
            pl.BlockSpec((None, D_MODEL, tf), lambda e, f, idx: (e, 0, f)),
            pl.BlockSpec((None, D_MODEL, tf), lambda e, f, idx: (e, 0, f)),
        ],
        out_specs=pl.BlockSpec((None, cap, tf), lambda e, f, idx: (e, 0, f)),
        scratch_shapes=[
            pltpu.VMEM((cap, D_MODEL), BF16),
            pltpu.VMEM((2, rows, D_MODEL), F32),
            pltpu.SemaphoreType.DMA((2,)),
        ],
    )
    return pl.pallas_call(
        functools.partial(_moe_up_body, cap=cap, rows=rows, row_chunk=min(ROW_CHUNK_UP, cap)),
        grid_spec=grid_spec,
        out_shape=jax.ShapeDtypeStruct((N_EXPERTS, cap, EXPERT_FF), BF16),
        compiler_params=_cparams(("arbitrary", "arbitrary")),
    )(idx_flat, h2, gate, w_gate, w_up)


def _moe_down_body(idx_ref, hid_ref, wd_ref, x_hbm, out_hbm, acc_ref, res_ref, sem, *, cap, rows, nkt):
    del x_hbm
    e = pl.program_id(0)
    rb = pl.program_id(1)
    kt = pl.program_id(2)
    last_kt = pl.num_programs(2) - 1
    base = e * cap + rb * rows
    gather_sem, scatter_sem = sem.at[0], sem.at[1]

    def wait_scatter():
        pltpu.make_async_copy(res_ref, out_hbm.at[pl.ds(0, rows), :], scatter_sem).wait()

    @pl.when(kt == 0)
    def _():
        @pl.when((e > 0) | (rb > 0))
        def _():
            wait_scatter()

        acc_ref[...] = jnp.zeros(acc_ref.shape, F32)

    seg = rows // nkt
    for u in range(seg):
        r = kt * seg + u
        pltpu.make_async_copy(out_hbm.at[pl.ds(idx_ref[base + r], 1), :], res_ref.at[pl.ds(r, 1), :],
                              gather_sem).start()

    acc_ref[...] += _dot(hid_ref[...], wd_ref[...].astype(BF16))

    @pl.when(kt == last_kt)
    def _():
        pltpu.make_async_copy(out_hbm.at[pl.ds(0, rows), :], res_ref, gather_sem).wait()
        res_ref[...] += acc_ref[...]

        def start(r, carry):
            pltpu.make_async_copy(res_ref.at[pl.ds(r, 1), :], out_hbm.at[pl.ds(idx_ref[base + r], 1), :],
                                  scatter_sem).start()
            return carry
        lax.fori_loop(0, rows, start, 0, unroll=DMA_UNROLL)

        @pl.when((e == pl.num_programs(0) - 1) & (rb == pl.num_programs(1) - 1))
        def _():
            wait_scatter()


def _moe_down(idx_flat, hid, w_down, x2, cap):
    n = x2.shape[0]
    rows = min(R_DOWN, cap)
    tk = TK_DOWN
    grid_spec = pltpu.PrefetchScalarGridSpec(
        num_scalar_prefetch=1,
        grid=(N_EXPERTS, cap // rows, EXPERT_FF // tk),
        in_specs=[
            pl.BlockSpec((None, rows, tk), lambda e, rb, kt, idx: (e, rb, kt)),
            pl.BlockSpec((None, tk, D_MODEL), lambda e, rb, kt, idx: (e, kt, 0)),
            pl.BlockSpec(memory_space=pl.ANY),
        ],
        out_specs=pl.BlockSpec(memory_space=pl.ANY),
        scratch_shapes=[
            pltpu.VMEM((rows, D_MODEL), F32),
            pltpu.VMEM((rows, D_MODEL), F32),
            pltpu.SemaphoreType.DMA((2,)),
        ],
    )
    return pl.pallas_call(
        functools.partial(_moe_down_body, cap=cap, rows=rows, nkt=EXPERT_FF // tk),
        grid_spec=grid_spec,
        out_shape=jax.ShapeDtypeStruct((n, D_MODEL), F32),
        input_output_aliases={3: 0},
        compiler_params=_cparams(("arbitrary", "arbitrary", "arbitrary")),
    )(idx_flat, hid, w_down, x2)


def _pad_rope_cols(t):
    half = QK_ROPE // 2
    z = jnp.zeros(t.shape[:-1] + (half,), t.dtype)
    return jnp.concatenate([t[..., :half], z, t[..., half:], z], axis=-1)


def _rope_tables_padded(seq):
    inv = ROPE_THETA ** (-jnp.arange(0, QK_ROPE, 2, dtype=F32) / QK_ROPE)
    ang = jnp.arange(seq, dtype=F32)[:, None] * inv[None, :]
    cos, sin = jnp.cos(ang), jnp.sin(ang)
    z = jnp.zeros_like(cos)
    return jnp.concatenate([cos, z, cos, z], axis=-1), jnp.concatenate([-sin, z, sin, z], axis=-1)


def _prepare_weights(w_in, mla_w_q_b, mla_w_kv_b, w_o, w_router, mla_q_rope_norm_g, mla_k_rope_norm_g):
    qkv_cols = 3 * A_WIDTH
    w_a = w_in[:, :qkv_cols].astype(BF16)
    w_ql = w_in[:, qkv_cols:qkv_cols + Q_LORA].astype(BF16)
    w_kvl = w_in[:, qkv_cols + Q_LORA:qkv_cols + Q_LORA + KV_LORA].astype(BF16)
    w_kr = _pad_rope_cols(w_in[:, qkv_cols + Q_LORA + KV_LORA:]).astype(BF16)
    wq = mla_w_q_b.reshape(Q_LORA, MLA_HEADS, QK_NOPE + QK_ROPE)
    wq_pad = jnp.concatenate([wq[..., :QK_NOPE], _pad_rope_cols(wq[..., QK_NOPE:])], axis=-1)
    wq_pad = wq_pad.reshape(Q_LORA, MLA_HEADS * MLA_HEAD_PAD).astype(BF16)
    wr = jnp.pad(w_router, ((0, 0), (0, LANES - N_EXPERTS)))
    wr1 = wr.astype(BF16)
    wr2 = (wr - wr1.astype(F32)).astype(BF16)
    return dict(
        w_a=w_a, w_ql=w_ql, w_kvl=w_kvl, w_kr=w_kr, wq_pad=wq_pad,
        wkv=mla_w_kv_b.astype(BF16), w_o=w_o.astype(BF16), wr1=wr1, wr2=wr2,
        gqr_pad=_pad_rope_cols(mla_q_rope_norm_g)[None, :], gkr_pad=_pad_rope_cols(mla_k_rope_norm_g)[None, :],
    )


def _encoder_layer(x3, pw, norm_mix_g, a_q_norm_g, a_k_norm_g, mla_q_a_norm_g, mla_kv_a_norm_g,
                   mla_q_nope_norm_g, mla_k_nope_norm_g, out_norm_a_g, out_norm_b_g, norm_ffn_g,
                   w_gate, w_up, w_down):
    b, s, d = x3.shape
    n = b * s
    x = x3.reshape(n, d)
    row = lambda v: v[None, :]
    cos_pad, sin_pad = _rope_tables_padded(s)

    qkv_by_dilation = _inproj_a(x, row(norm_mix_g), pw["w_a"], row(a_q_norm_g), row(a_k_norm_g), b, s)
    branches = [_dilated_branch(qkv) for qkv in qkv_by_dilation]

    cq, ckv, kr = _inproj_lat(x, row(norm_mix_g), pw["w_ql"], pw["w_kvl"], pw["w_kr"], row(mla_q_a_norm_g),
                              row(mla_kv_a_norm_g), pw["gkr_pad"], cos_pad, sin_pad, s)
    qm, km, vm = _mla_up(cq, ckv, kr, pw["wq_pad"], pw["wkv"], row(mla_q_nope_norm_g), pw["gqr_pad"],
                         row(mla_k_nope_norm_g), cos_pad, sin_pad, s)
    wide = MLA_HEADS * MLA_HEAD_PAD
    ob = _mla_flash(qm.reshape(b, s, wide), km.reshape(b, s, wide), vm.reshape(b, s, wide))

    x2 = _outproj(branches, ob.reshape(n, MLA_WIDTH), row(out_norm_a_g), row(out_norm_b_g), pw["w_o"], x, s)

    h2, aff_t = _router(x2, row(norm_ffn_g), pw["wr1"], pw["wr2"])
    cap = CAPACITY_FACTOR * n // N_EXPERTS
    idx, gate = _topk(aff_t, cap)
    idx_flat = idx.reshape(N_EXPERTS * cap)
    hid = _moe_up(idx_flat, h2, gate, w_gate, w_up, cap)
    y = _moe_down(idx_flat, hid, w_down, x2, cap)
    return y.reshape(b, s, d)


def kernel(x_prompt, x_sample, norm_mix_g, w_in, a_q_norm_g, a_k_norm_g, mla_q_a_norm_g, mla_w_q_b, mla_kv_a_norm_g, mla_w_kv_b, mla_q_nope_norm_g, mla_q_rope_norm_g, mla_k_nope_norm_g, mla_k_rope_norm_g, out_norm_a_g, out_norm_b_g, w_o, norm_ffn_g, w_router, w_gate, w_up, w_down):
    depth = w_in.shape[0]
    prepared = [_prepare_weights(w_in[l], mla_w_q_b[l], mla_w_kv_b[l], w_o[l], w_router[l],
                                 mla_q_rope_norm_g[l], mla_k_rope_norm_g[l]) for l in range(depth)]

    def run_trunk(x):
        for l in range(depth):
            pw = prepared[l]
            x = _encoder_layer(x, pw, norm_mix_g[l], a_q_norm_g[l], a_k_norm_g[l], mla_q_a_norm_g[l],
                               mla_kv_a_norm_g[l], mla_q_nope_norm_g[l], mla_k_nope_norm_g[l],
                               out_norm_a_g[l], out_norm_b_g[l], norm_ffn_g[l], w_gate[l], w_up[l], w_down[l])
        return x

    return run_trunk(x_prompt), run_trunk(x_sample)
```

```python
import functools
import math

import jax
import jax.numpy as jnp
from jax import lax
from jax.experimental import pallas as pl
from jax.experimental.pallas import tpu as pltpu

F32 = jnp.float32
BF16 = jnp.bfloat16

D_MODEL = 4096
A_HEADS = 16
A_HEAD_DIM = 128
A_WIDTH = A_HEADS * A_HEAD_DIM
DILATED_PATTERNS = ((128, 1), (512, 4), (2048, 16))
DILATIONS = tuple(d for _, d in DILATED_PATTERNS)
MLA_HEADS = 16
QK_NOPE = 128
QK_ROPE = 64
V_HEAD = 128
Q_LORA = 1024
KV_LORA = 512
MLA_WIDTH = MLA_HEADS * V_HEAD
ROPE_THETA = 10000.0
N_EXPERTS = 16
EXPERT_FF = 2048
CAPACITY_FACTOR = 2
EPS = 1e-6

LANES = 128
MLA_HEAD_PAD = 2 * LANES
N_SIDE = 64

TM_INPROJ = 512
TN_INPROJ = 512
TM_LAT = 256
TM_MLAUP = 256
TQ_MLA = 1024
TK_MLA = 2048
TQ_DIL = 128
TB_DIL = 512
TM_OUT = 512
TN_OUT = 512
TM_ROUTER = 256
TF_MOE = 256
ROW_CHUNK_UP = 512
TK_DOWN = 256
R_DOWN = 1024
G_GATHER = 256
DMA_UNROLL = 8
VMEM_LIMIT = 56 * 1024 * 1024


def _cparams(sem):
    return pltpu.CompilerParams(dimension_semantics=sem, vmem_limit_bytes=VMEM_LIMIT)


def _rms(x, g):
    ms = jnp.mean(x * x, axis=-1, keepdims=True)
    return x * lax.rsqrt(ms + EPS) * g


def _dot(a, b):
    return jnp.dot(a, b, preferred_element_type=F32)


def _dot_nt(a, b):
    return lax.dot_general(a, b, (((1,), (1,)), ((), ())), preferred_element_type=F32)


def _inproj_a_body(x_ref, g_ref, w_ref, gq_ref, gk_ref, *refs, chunk, ncol_tiles):
    nd = len(DILATIONS)
    out_refs = refs[:nd]
    xn_ref, y0_ref, y1_ref, z_ref = refs[nd:]
    t = pl.program_id(0)
    j = t % ncol_tiles
    jp = (t + ncol_tiles - 1) % ncol_tiles
    tm = x_ref.shape[0]
    tn = w_ref.shape[1]
    nq = A_WIDTH // tn
    planes = tn // LANES

    @pl.when(t == 0)
    def _():
        y1_ref[...] = jnp.zeros(y1_ref.shape, F32)

    @pl.when(j == 0)
    def _():
        def norm_chunk(c, carry):
            r = pl.multiple_of(c * chunk, chunk)
            xn_ref[pl.ds(r, chunk), :] = _rms(x_ref[pl.ds(r, chunk), :], g_ref[...]).astype(BF16)
            return carry

        lax.fori_loop(0, tm // chunk, norm_chunk, 0)

    is_q = jp < nq
    is_k = (jp >= nq) & (jp < 2 * nq)
    q_gain = gq_ref[...] * (A_HEAD_DIM ** -0.5 * math.log2(math.e))
    gain = jnp.where(is_q, q_gain, jnp.where(is_k, gk_ref[...], jnp.ones_like(q_gain)))

    def work(y_cur, y_prev):
        y_cur[...] = _dot(xn_ref[...], w_ref[...])
        for c in range(planes):
            yh = y_prev[:, c * LANES:(c + 1) * LANES]
            ms = jnp.mean(yh * yh, axis=-1, keepdims=True)
            factor = jnp.where(is_q | is_k, lax.rsqrt(ms + EPS), jnp.ones_like(ms))
            z_ref[c] = yh * factor * gain
        for o_ref, d in zip(out_refs, DILATIONS):
            for c in range(planes):
                cols = slice(c * LANES, (c + 1) * LANES)
                if d == 1:
                    o_ref[:, cols] = z_ref[c].astype(BF16)
                else:
                    for r in range(d):
                        o_ref[r, :, cols] = z_ref.at[c][pl.ds(r, tm // d, stride=d), :].astype(BF16)

    @pl.when(t % 2 == 0)
    def _():
        work(y0_ref, y1_ref)

    @pl.when(t % 2 == 1)
    def _():
        work(y1_ref, y0_ref)


def _inproj_a(x, g, w_a, gq, gk, batch, seq):
    n = x.shape[0]
    tm = min(TM_INPROJ, seq)
    tn = TN_INPROJ
    ncol = w_a.shape[1]
    nseq = seq // tm
    nrow_tiles = n // tm
    nct = ncol // tn
    steps = nrow_tiles * nct + 1

    def prev_tile(t):
        tp = jnp.maximum(t - 1, 0)
        return tp // nct, tp % nct

    def residue_major(t):
        i, j = prev_tile(t)
        return i // nseq, 0, i % nseq, j

    out_specs, out_shape = [], []
    for d in DILATIONS:
        if d == 1:
            out_specs.append(pl.BlockSpec((tm, tn), prev_tile))
            out_shape.append(jax.ShapeDtypeStruct((n, ncol), BF16))
        else:
            out_specs.append(pl.BlockSpec((None, d, tm // d, tn), residue_major))
            out_shape.append(jax.ShapeDtypeStruct((batch, d, seq // d, ncol), BF16))
    outs = pl.pallas_call(
        functools.partial(_inproj_a_body, chunk=128, ncol_tiles=nct),
        grid=(steps,),
        in_specs=[
            pl.BlockSpec((tm, D_MODEL), lambda t: (jnp.minimum(t // nct, nrow_tiles - 1), 0)),
            pl.BlockSpec((1, D_MODEL), lambda t: (0, 0)),
            pl.BlockSpec((D_MODEL, tn), lambda t: (0, t % nct)),
            pl.BlockSpec((1, A_HEAD_DIM), lambda t: (0, 0)),
            pl.BlockSpec((1, A_HEAD_DIM), lambda t: (0, 0)),
        ],
        out_specs=out_specs,
        out_shape=out_shape,
        scratch_shapes=[
            pltpu.VMEM((tm, D_MODEL), BF16),
            pltpu.VMEM((tm, tn), F32),
            pltpu.VMEM((tm, tn), F32),
            pltpu.VMEM((tn // LANES, tm, LANES), F32),
        ],
        compiler_params=_cparams(("arbitrary",)),
    )(x, g, w_a, gq, gk)
    return [o.reshape(batch, d, seq // d, ncol) for o, d in zip(outs, DILATIONS)]


def _rope_padded(t, cos, sin):
    return t * cos + pltpu.roll(t, 2 * (QK_ROPE // 2), 1) * sin


def _inproj_lat_body(x_ref, g_ref, wq_ref, wkv_ref, wkr_ref, gq_ref, gkv_ref, gkr_ref, cos_ref, sin_ref,
                     cq_ref, ckv_ref, kr_ref):
    xn = _rms(x_ref[...], g_ref[...]).astype(BF16)
    cq_ref[...] = _rms(_dot(xn, wq_ref[...]), gq_ref[...]).astype(BF16)
    ckv_ref[...] = _rms(_dot(xn, wkv_ref[...]), gkv_ref[...]).astype(BF16)
    kr = _dot(xn, wkr_ref[...])
    ms = jnp.sum(kr * kr, axis=-1, keepdims=True) * (1.0 / QK_ROPE)
    krn = kr * lax.rsqrt(ms + EPS) * gkr_ref[...]
    kr_ref[...] = _rope_padded(krn, cos_ref[...], sin_ref[...]).astype(BF16)


def _inproj_lat(x, g, wq, wkv, wkr, gq, gkv, gkr, cos_pad, sin_pad, seq):
    n = x.shape[0]
    tm = min(TM_LAT, seq)
    nseq = seq // tm
    const = lambda i: (0, 0)
    return pl.pallas_call(
        _inproj_lat_body,
        grid=(n // tm,),
        in_specs=[
            pl.BlockSpec((tm, D_MODEL), lambda i: (i, 0)),
            pl.BlockSpec((1, D_MODEL), const),
            pl.BlockSpec((D_MODEL, Q_LORA), const),
            pl.BlockSpec((D_MODEL, KV_LORA), const),
            pl.BlockSpec((D_MODEL, LANES), const),
            pl.BlockSpec((1, Q_LORA), const),
            pl.BlockSpec((1, KV_LORA), const),
            pl.BlockSpec((1, LANES), const),
            pl.BlockSpec((tm, LANES), lambda i: (i % nseq, 0)),
            pl.BlockSpec((tm, LANES), lambda i: (i % nseq, 0)),
        ],
        out_specs=[
            pl.BlockSpec((tm, Q_LORA), lambda i: (i, 0)),
            pl.BlockSpec((tm, KV_LORA), lambda i: (i, 0)),
            pl.BlockSpec((tm, LANES), lambda i: (i, 0)),
        ],
        out_shape=[
            jax.ShapeDtypeStruct((n, Q_LORA), BF16),
            jax.ShapeDtypeStruct((n, KV_LORA), BF16),
            jax.ShapeDtypeStruct((n, LANES), BF16),
        ],
        compiler_params=_cparams(("parallel",)),
    )(x, g, wq, wkv, wkr, gq, gkv, gkr, cos_pad, sin_pad)


def _mla_up_body(cq_ref, ckv_ref, kr_ref, wq_ref, wkv_ref, gqn_ref, gqr_ref, gkn_ref, cos_ref, sin_ref,
                 q_ref, k_ref, v_ref):
    cq = cq_ref[...]
    ckv = ckv_ref[...]
    kr = kr_ref[...]
    cos = cos_ref[...]
    sin = sin_ref[...]
    scale = (QK_NOPE + QK_ROPE) ** -0.5 * math.log2(math.e)
    ones = jnp.ones((cq.shape[0], V_HEAD), BF16)
    for h in range(MLA_HEADS):
        lo = h * MLA_HEAD_PAD
        q = _dot(cq, wq_ref[:, lo:lo + MLA_HEAD_PAD])
        qn = _rms(q[:, :LANES], gqn_ref[...] * scale)
        qr = q[:, LANES:]
        ms = jnp.sum(qr * qr, axis=-1, keepdims=True) * (1.0 / QK_ROPE)
        qr = _rope_padded(qr * lax.rsqrt(ms + EPS) * (gqr_ref[...] * scale), cos, sin)
        q_ref[:, lo:lo + LANES] = qn.astype(BF16)
        q_ref[:, lo + LANES:lo + MLA_HEAD_PAD] = qr.astype(BF16)
        kv = _dot(ckv, wkv_ref[:, lo:lo + MLA_HEAD_PAD])
        k_ref[:, lo:lo + LANES] = _rms(kv[:, :LANES], gkn_ref[...]).astype(BF16)
        k_ref[:, lo + LANES:lo + MLA_HEAD_PAD] = kr
        v_ref[:, lo:lo + V_HEAD] = kv[:, LANES:].astype(BF16)
        v_ref[:, lo + V_HEAD:lo + 2 * V_HEAD] = ones


def _mla_up(cq, ckv, kr, wq_pad, wkv, gqn, gqr_pad, gkn, cos_pad, sin_pad, seq):
    n = cq.shape[0]
    tm = min(TM_MLAUP, seq)
    nseq = seq // tm
    const = lambda i: (0, 0)
    row = lambda i: (i, 0)
    wide = MLA_HEADS * MLA_HEAD_PAD
    return pl.pallas_call(
        _mla_up_body,
        grid=(n // tm,),
        in_specs=[
            pl.BlockSpec((tm, Q_LORA), row),
            pl.BlockSpec((tm, KV_LORA), row),
            pl.BlockSpec((tm, LANES), row),
            pl.BlockSpec((Q_LORA, wide), const),
            pl.BlockSpec((KV_LORA, wide), const),
            pl.BlockSpec((1, LANES), const),
            pl.BlockSpec((1, LANES), const),
            pl.BlockSpec((1, LANES), const),
            pl.BlockSpec((tm, LANES), lambda i: (i % nseq, 0)),
            pl.BlockSpec((tm, LANES), lambda i: (i % nseq, 0)),
        ],
        out_specs=[
            pl.BlockSpec((tm, wide), row),
            pl.BlockSpec((tm, wide), row),
            pl.BlockSpec((tm, wide), row),
        ],
        out_shape=[
            jax.ShapeDtypeStruct((n, wide), BF16),
            jax.ShapeDtypeStruct((n, wide), BF16),
            jax.ShapeDtypeStruct((n, wide), BF16),
        ],
        compiler_params=_cparams(("parallel",)),
    )(cq, ckv, kr, wq_pad, wkv, gqn, gqr_pad, gkn, cos_pad, sin_pad)


def _mla_flash_body(q_ref, k_ref, v_ref, o_ref, m_ref, acc_ref, s0_ref, s1_ref, *, tk):
    nk = k_ref.shape[0] // tk
    m_ref[...] = jnp.full(m_ref.shape, -jnp.inf, F32)
    acc_ref[...] = jnp.zeros(acc_ref.shape, F32)

    def scores(c, s_ref):
        r = pl.multiple_of(c * tk, tk)
        s_ref[...] = _dot_nt(q_ref[...], k_ref[pl.ds(r, tk), :])

    def update(c, s_ref):
        r = pl.multiple_of(c * tk, tk)
        s = s_ref[...]
        m_old = m_ref[...]
        m_new = jnp.maximum(m_old, jnp.max(s, axis=-1, keepdims=True))
        alpha = jnp.exp2(m_old - m_new)
        p = jnp.exp2((s - jnp.tile(m_new, (1, tk // LANES))).astype(BF16))
        acc_ref[...] = jnp.tile(alpha, (1, 2)) * acc_ref[...] + _dot(p, v_ref[pl.ds(r, tk), :])
        m_ref[...] = m_new

    scores(0, s0_ref)

    def pair(jj, carry):
        c = 2 * jj
        scores(c + 1, s1_ref)
        update(c, s0_ref)
        scores(c + 2, s0_ref)
        update(c + 1, s1_ref)
        return carry

    lax.fori_loop(0, nk // 2 - 1, pair, 0)
    scores(nk - 1, s1_ref)
    update(nk - 2, s0_ref)
    update(nk - 1, s1_ref)
    o_ref[...] = (acc_ref[:, :V_HEAD] / acc_ref[:, V_HEAD:]).astype(o_ref.dtype)


def _mla_flash(qm, km, vm):
    b, s, _ = qm.shape
    tq = min(TQ_MLA, s)
    tk = min(TK_MLA, s // 2)
    return pl.pallas_call(
        functools.partial(_mla_flash_body, tk=tk),
        grid=(b, MLA_HEADS, s // tq),
        in_specs=[
            pl.BlockSpec((None, tq, MLA_HEAD_PAD), lambda bi, h, i: (bi, i, h)),
            pl.BlockSpec((None, s, MLA_HEAD_PAD), lambda bi, h, i: (bi, 0, h)),
            pl.BlockSpec((None, s, 2 * V_HEAD), lambda bi, h, i: (bi, 0, h)),
        ],
        out_specs=pl.BlockSpec((None, tq, V_HEAD), lambda bi, h, i: (bi, i, h)),
        out_shape=jax.ShapeDtypeStruct((b, s, MLA_WIDTH), BF16),
        scratch_shapes=[pltpu.VMEM((tq, LANES), F32), pltpu.VMEM((tq, 2 * V_HEAD), F32),
                        pltpu.VMEM((tq, tk), F32), pltpu.VMEM((tq, tk), F32)],
        compiler_params=_cparams(("parallel", "parallel", "arbitrary")),
    )(qm, km, vm)


def _dilated_body(q_ref, k0_ref, kp_ref, kn_ref, v0_ref, vp_ref, vn_ref, o_ref, lse_ref, kw_ref, vw_ref,
                  s_ref, *, dil, nq, uq):
    i = pl.program_id(2)
    tb = q_ref.shape[0]
    units = tb // uq
    kw_ref[0:N_SIDE, :] = kp_ref[...]
    kw_ref[N_SIDE:N_SIDE + tb, :] = k0_ref[...]
    kw_ref[N_SIDE + tb:, :] = kn_ref[...]
    vw_ref[0:N_SIDE, :] = vp_ref[...]
    vw_ref[N_SIDE:N_SIDE + tb, :] = v0_ref[...]
    vw_ref[N_SIDE + tb:, :] = vn_ref[...]

    @pl.when(i == nq - 1)
    def _():
        kw_ref[N_SIDE + tb:, :] = jnp.zeros((N_SIDE, A_WIDTH), BF16)
        vw_ref[N_SIDE + tb:, :] = jnp.zeros((N_SIDE, A_WIDTH), BF16)

    win = uq + 2 * N_SIDE
    row = lax.broadcasted_iota(jnp.int32, (uq, win), 0)
    col = lax.broadcasted_iota(jnp.int32, (uq, win), 1)
    rel = col - N_SIDE - row
    arel = jnp.abs(rel)
    near = arel <= N_SIDE
    token_dist = (dil * arel).astype(F32)
    lane = lax.broadcasted_iota(jnp.int32, (uq, LANES), 1)
    ones = jnp.ones((win, LANES), BF16)

    for u in range(units):
        valid = near
        if u == 0:
            valid = valid & ((rel >= 0) | (row >= N_SIDE) | (i > 0))
        if u == units - 1:
            valid = valid & ((rel < N_SIDE) | (row < uq - N_SIDE) | (i < nq - 1))
        dist = jnp.where(valid, token_dist, jnp.inf)
        qrows = slice(u * uq, (u + 1) * uq)
        wrows = slice(u * uq, u * uq + win)
        sbuf = s_ref.at[u % 2]
        for h in range(A_HEADS):
            sl = slice(h * A_HEAD_DIM, (h + 1) * A_HEAD_DIM)
            sbuf[h] = _dot_nt(q_ref[qrows, sl], kw_ref[wrows, sl])
        lse_all = jnp.zeros((uq, LANES), F32)
        for h in range(A_HEADS):
            sl = slice(h * A_HEAD_DIM, (h + 1) * A_HEAD_DIM)
            slope = 2.0 ** (-8.0 * (h + 1) / A_HEADS) * math.log2(math.e)
            s = sbuf[h] - slope * dist
            m = jnp.max(s, axis=-1, keepdims=True)
            p = jnp.exp2((s - m).astype(BF16))
            l = _dot(p, ones)
            o_ref[qrows, sl] = (_dot(p, vw_ref[wrows, sl]) / l).astype(o_ref.dtype)
            lse_all = jnp.where(lane == h, m + jnp.log2(l), lse_all)
        lse_ref[qrows, :] = lse_all


def _dilated_branch(qkv):
    b, dil, sub, _ = qkv.shape
    uq = TQ_DIL
    tq = min(TB_DIL, sub)
    nq = sub // tq
    nhalo = sub // N_SIDE
    per = tq // N_SIDE

    def own(which):
        return pl.BlockSpec((None, None, tq, A_WIDTH), lambda bi, r, i: (bi, r, i, which))

    def prev(which):
        return pl.BlockSpec((None, None, N_SIDE, A_WIDTH),
                            lambda bi, r, i: (bi, r, jnp.maximum(per * i - 1, 0), which))

    def nxt(which):
        return pl.BlockSpec((None, None, N_SIDE, A_WIDTH),
                            lambda bi, r, i: (bi, r, jnp.minimum(per * (i + 1), nhalo - 1), which))

    return pl.pallas_call(
        functools.partial(_dilated_body, dil=dil, nq=nq, uq=uq),
        grid=(b, dil, nq),
        in_specs=[own(0), own(1), prev(1), nxt(1), own(2), prev(2), nxt(2)],
        out_specs=[
            pl.BlockSpec((None, None, tq, A_WIDTH), lambda bi, r, i: (bi, r, i, 0)),
            pl.BlockSpec((None, None, tq, LANES), lambda bi, r, i: (bi, r, i, 0)),
        ],
        out_shape=[
            jax.ShapeDtypeStruct((b, dil, sub, A_WIDTH), BF16),
            jax.ShapeDtypeStruct((b, dil, sub, LANES), F32),
        ],
        scratch_shapes=[
            pltpu.VMEM((tq + 2 * N_SIDE, A_WIDTH), BF16),
            pltpu.VMEM((tq + 2 * N_SIDE, A_WIDTH), BF16),
            pltpu.VMEM((2, A_HEADS, uq, uq + 2 * N_SIDE), F32),
        ],
        compiler_params=_cparams(("parallel", "parallel", "arbitrary")),
    )(qkv, qkv, qkv, qkv, qkv, qkv, qkv)


def _outproj_body(*refs):
    nd = len(DILATIONS)
    o_refs, l_refs = refs[:nd], refs[nd:2 * nd]
    ob_ref, ga_ref, gb_ref, w_ref, x_ref, y_ref, mix_ref, oa_ref, t_ref, lt_ref = refs[2 * nd:]
    j = pl.program_id(1)
    tm = x_ref.shape[0]

    def rows_of(d, r):
        return slice(None) if d == 1 else pl.ds(r, tm // d, stride=d)

    @pl.when(j == 0)
    def _():
        for i, d in enumerate(DILATIONS):
            for r in range(d):
                lt_ref.at[i][rows_of(d, r), :] = l_refs[i][r]
        ls = [lt_ref[i] for i in range(nd)]
        mx = functools.reduce(jnp.maximum, ls)
        es = [jnp.exp2(l - mx) for l in ls]
        inv = 1.0 / functools.reduce(lambda a, b: a + b, es)
        head_of_col = lax.broadcasted_iota(jnp.int32, (2 * LANES, A_WIDTH), 1) // A_HEAD_DIM
        lane_of_row = lax.broadcasted_iota(jnp.int32, (2 * LANES, A_WIDTH), 0) % LANES
        expand = (head_of_col == lane_of_row).astype(BF16)
        for i, d in enumerate(DILATIONS):
            w = es[i] * inv
            hi = w.astype(BF16)
            lo = (w - hi.astype(F32)).astype(BF16)
            wexp = _dot(jnp.concatenate([hi, lo], axis=1), expand)
            for h in range(A_HEADS):
                sl = slice(h * A_HEAD_DIM, (h + 1) * A_HEAD_DIM)
                for r in range(d):
                    t_ref.at[h][rows_of(d, r), :] = o_refs[i][r, :, sl].astype(F32)
                if i == 0:
                    oa_ref[h] = wexp[:, sl] * t_ref[h]
                else:
                    oa_ref[h] += wexp[:, sl] * t_ref[h]
        ss = jnp.zeros((tm, 1), F32)
        for h in range(A_HEADS):
            oa = oa_ref[h]
            ss = ss + jnp.sum(oa * oa, axis=-1, keepdims=True)
        inv_a = lax.rsqrt(ss * (1.0 / A_WIDTH) + EPS)
        for h in range(A_HEADS):
            sl = slice(h * A_HEAD_DIM, (h + 1) * A_HEAD_DIM)
            mix_ref[:, sl] = (oa_ref[h] * inv_a * ga_ref[:, sl]).astype(BF16)
        mix_ref[:, A_WIDTH:] = _rms(ob_ref[...].astype(F32), gb_ref[...]).astype(BF16)

    y_ref[...] = x_ref[...] + _dot(mix_ref[...], w_ref[...])


def _outproj(branches, ob, ga, gb, w_o, x, seq):
    n = x.shape[0]
    tm = min(TM_OUT, seq)
    tn = TN_OUT
    nseq = seq // tm
    row = lambda i, j: (i, 0)
    const = lambda i, j: (0, 0)
    mixw = A_WIDTH + MLA_WIDTH
    res = lambda i, j: (i // nseq, 0, i % nseq, 0)
    o_specs = [pl.BlockSpec((None, d, tm // d, A_WIDTH), res) for d in DILATIONS]
    l_specs = [pl.BlockSpec((None, d, tm // d, LANES), res) for d in DILATIONS]
    return pl.pallas_call(
        _outproj_body,
        grid=(n // tm, D_MODEL // tn),
        in_specs=o_specs + l_specs + [
            pl.BlockSpec((tm, MLA_WIDTH), row),
            pl.BlockSpec((1, A_WIDTH), const), pl.BlockSpec((1, MLA_WIDTH), const),
            pl.BlockSpec((mixw, tn), lambda i, j: (0, j)),
            pl.BlockSpec((tm, tn), lambda i, j: (i, j)),
        ],
        out_specs=pl.BlockSpec((tm, tn), lambda i, j: (i, j)),
        out_shape=jax.ShapeDtypeStruct((n, D_MODEL), F32),
        scratch_shapes=[
            pltpu.VMEM((tm, mixw), BF16),
            pltpu.VMEM((A_HEADS, tm, A_HEAD_DIM), F32),
            pltpu.VMEM((A_HEADS, tm, A_HEAD_DIM), F32),
            pltpu.VMEM((len(DILATIONS), tm, LANES), F32),
        ],
        compiler_params=_cparams(("parallel", "arbitrary")),
    )(*[o for o, _ in branches], *[l for _, l in branches], ob, ga, gb, w_o, x)


def _router_body(x_ref, g_ref, w1_ref, w2_ref, h_ref, aff_ref):
    h = _rms(x_ref[...], g_ref[...])
    half = D_MODEL // 2
    lo = pltpu.bitcast(h[:, :half].astype(BF16).astype(F32), jnp.uint32)
    hi = pltpu.bitcast(h[:, half:].astype(BF16).astype(F32), jnp.uint32)
    h_ref[...] = hi | (lo >> 16)
    h1 = h.astype(BF16)
    h2 = (h - h1.astype(F32)).astype(BF16)
    logits = _dot(h1, w1_ref[...]) + _dot(h1, w2_ref[...]) + _dot(h2, w1_ref[...])
    lane = lax.broadcasted_iota(jnp.int32, logits.shape, 1)
    logits = jnp.where(lane < N_EXPERTS, logits, -jnp.inf)
    m = jnp.max(logits, axis=-1, keepdims=True)
    e = jnp.exp(logits - m)
    aff_ref[...] = e / jnp.sum(e, axis=-1, keepdims=True)


def _router(x2, g, wr1, wr2):
    n = x2.shape[0]
    tm = min(TM_ROUTER, n)
    h2, aff = pl.pallas_call(
        _router_body,
        grid=(n // tm,),
        in_specs=[
            pl.BlockSpec((tm, D_MODEL), lambda i: (i, 0)),
            pl.BlockSpec((1, D_MODEL), lambda i: (0, 0)),
            pl.BlockSpec((D_MODEL, LANES), lambda i: (0, 0)),
            pl.BlockSpec((D_MODEL, LANES), lambda i: (0, 0)),
        ],
        out_specs=[
            pl.BlockSpec((tm, D_MODEL // 2), lambda i: (i, 0)),
            pl.BlockSpec((tm, LANES), lambda i: (i, 0)),
        ],
        out_shape=[
            jax.ShapeDtypeStruct((n, D_MODEL // 2), jnp.uint32),
            jax.ShapeDtypeStruct((n, LANES), F32),
        ],
        compiler_params=_cparams(("parallel",)),
    )(x2, g, wr1, wr2)
    return h2, aff[:, :N_EXPERTS].T


def _sum_all(x):
    return jnp.sum(jnp.sum(x, axis=1, keepdims=True), axis=0, keepdims=True)


def _topk_body(a_ref, idx_ref, gate_ref, *, cap):
    a = a_ref[...]
    nc = a.shape[0]
    bits = pltpu.bitcast(a, jnp.int32)
    thr = jnp.zeros((1, 1), jnp.int32)
    for bit in range(30, -1, -1):
        cand = thr + (1 << bit)
        cnt = _sum_all((bits >= cand).astype(F32))
        thr = jnp.where(cnt >= cap, cand, thr)
    gt = bits > thr
    eq = bits == thr
    eqf = eq.astype(F32)
    need = cap - _sum_all(gt.astype(F32))

    li = lax.broadcasted_iota(jnp.int32, (LANES, LANES), 0)
    lj = lax.broadcasted_iota(jnp.int32, (LANES, LANES), 1)
    upper_incl = (li <= lj).astype(BF16)
    ci = lax.broadcasted_iota(jnp.int32, (nc, nc), 0)
    cj = lax.broadcasted_iota(jnp.int32, (nc, nc), 1)
    lower_strict = (cj < ci).astype(BF16)
    upper_incl_c = (ci <= cj).astype(BF16)

    p_eq = _dot(eqf.astype(BF16), upper_incl)
    tot_eq = jnp.broadcast_to(p_eq[:, LANES - 1:LANES], (nc, LANES))
    off_eq = _dot(lower_strict, tot_eq.astype(BF16))
    sel = gt | (eq & ((p_eq - eqf + off_eq) < need))
    selb = sel.astype(F32).astype(BF16)

    p_sel = _dot(selb, upper_incl)
    tot_row = _dot_nt(jnp.ones((8, LANES), BF16), selb)
    offi_row = _dot(tot_row.astype(BF16), upper_incl_c)[0:1, :]
    offe_row = offi_row - tot_row[0:1, :]

    slot_c = lax.broadcasted_iota(jnp.int32, (cap, nc), 0).astype(F32)
    chunk_of = jnp.sum((offi_row <= slot_c).astype(F32), axis=1, keepdims=True)
    onehot = lax.broadcasted_iota(jnp.int32, (cap, nc), 1).astype(F32) == chunk_of
    onehot_b = onehot.astype(F32).astype(BF16)
    run = _dot(onehot_b, p_sel.astype(BF16))
    offe = jnp.sum(jnp.where(onehot, offe_row, 0.0), axis=1, keepdims=True)
    slot_l = lax.broadcasted_iota(jnp.int32, (cap, LANES), 0).astype(F32)
    lane_of = jnp.sum(((run + offe) <= slot_l).astype(F32), axis=1, keepdims=True)
    idx_ref[...] = (chunk_of * LANES + lane_of).astype(jnp.int32)

    a1 = a.astype(BF16)
    r1 = a - a1.astype(F32)
    a2 = r1.astype(BF16)
    a3 = (r1 - a2.astype(F32)).astype(BF16)
    rows = _dot(onehot_b, a1) + _dot(onehot_b, a2) + _dot(onehot_b, a3)
    lanes = lax.broadcasted_iota(jnp.int32, (cap, LANES), 1).astype(F32)
    gate_ref[...] = jnp.sum(jnp.where(lanes == lane_of, rows, 0.0), axis=1, keepdims=True)


def _topk(aff_t, cap):
    e, n = aff_t.shape
    nc = n // LANES
    return pl.pallas_call(
        functools.partial(_topk_body, cap=cap),
        grid=(e,),
        in_specs=[pl.BlockSpec((None, nc, LANES), lambda i: (i, 0, 0))],
        out_specs=[
            pl.BlockSpec((None, cap, 1), lambda i: (i, 0, 0)),
            pl.BlockSpec((None, cap, 1), lambda i: (i, 0, 0)),
        ],
        out_shape=[
            jax.ShapeDtypeStruct((e, cap, 1), jnp.int32),
            jax.ShapeDtypeStruct((e, cap, 1), F32),
        ],
        compiler_params=_cparams(("parallel",)),
    )(aff_t.reshape(e, nc, LANES))


def _row_copy(src_hbm, tok, dst_ref, r, sem):
    return pltpu.make_async_copy(src_hbm.at[pl.ds(tok, 1), :], dst_ref.at[pl.ds(r, 1), :], sem)


def _moe_up_body(idx_ref, h_hbm, gate_ref, wg_ref, wu_ref, hid_ref, xe_ref, stage_ref, sem, *, cap, rows,
                 row_chunk):
    e = pl.program_id(0)
    f = pl.program_id(1)

    @pl.when(f == 0)
    def _():
        nchunk = cap // rows

        def start(c, slot):
            def body(r, carry):
                tok = idx_ref[e * cap + c * rows + r]
                _row_copy(h_hbm, tok, stage_ref.at[slot], r, sem.at[slot]).start()
                return carry
            lax.fori_loop(0, rows, body, 0, unroll=DMA_UNROLL)

        def wait(slot):
            pltpu.make_async_copy(h_hbm.at[pl.ds(0, rows), :], stage_ref.at[slot], sem.at[slot]).wait()

        start(0, 0)
        for c in range(nchunk):
            slot = c % 2
            if c + 1 < nchunk:
                start(c + 1, 1 - slot)
            wait(slot)
            words = stage_ref[slot]
            half = D_MODEL // 2
            xe_ref[c * rows:(c + 1) * rows, :half] = pltpu.bitcast(words << 16, F32).astype(BF16)
            xe_ref[c * rows:(c + 1) * rows, half:] = pltpu.bitcast(
                words & jnp.uint32(0xFFFF0000), F32).astype(BF16)

    wg = wg_ref[...].astype(BF16)
    wu = wu_ref[...].astype(BF16)
    for c in range(cap // row_chunk):
        rs = slice(c * row_chunk, (c + 1) * row_chunk)
        xe = xe_ref[rs, :]
        g = _dot(xe, wg)
        u = _dot(xe, wu)
        hid_ref[rs, :] = (g / (1.0 + jnp.exp(-g)) * u * gate_ref[rs, :]).astype(BF16)


def _moe_up(idx_flat, h2, gate, w_gate, w_up, cap):
    n = h2.shape[0]
    tf = TF_MOE
    rows = min(G_GATHER, cap)
    grid_spec = pltpu.PrefetchScalarGridSpec(
        num_scalar_prefetch=1,
        grid=(N_EXPERTS, EXPERT_FF // tf),
        in_specs=[
            pl.BlockSpec(memory_space=pl.ANY),
            pl.BlockSpec((None, cap, 1), lambda e, f, idx: (e, 0, 0)),
            pl.BlockSpec((None, D_MODEL, tf), lambda e, f, idx: (e, 0, f)),
            pl.BlockSpec((None, D_MODEL, tf), lambda e, f, idx: (e, 0, f)),
        ],
        out_specs=pl.BlockSpec((None, cap, tf), lambda e, f, idx: (e, 0, f)),
        scratch_shapes=[
            pltpu.VMEM((cap, D_MODEL), BF16),
            pltpu.VMEM((2, rows, D_MODEL // 2), jnp.uint32),
            pltpu.SemaphoreType.DMA((2,)),
        ],
    )
    return pl.pallas_call(
        functools.partial(_moe_up_body, cap=cap, rows=rows, row_chunk=min(ROW_CHUNK_UP, cap)),
        grid_spec=grid_spec,
        out_shape=jax.ShapeDtypeStruct((N_EXPERTS, cap, EXPERT_FF), BF16),
        compiler_params=_cparams(("arbitrary", "arbitrary")),
    )(idx_flat, h2, gate, w_gate, w_up)


def _moe_down_body(idx_ref, hid_ref, wd_ref, x_hbm, out_hbm, acc_ref, res_ref, sem, *, cap, rows, nkt):
    del x_hbm
    e = pl.program_id(0)
    rb = pl.program_id(1)
    kt = pl.program_id(2)
    last_kt = pl.num_programs(2) - 1
    base = e * cap + rb * rows
    gather_sem, scatter_sem = sem.at[0], sem.at[1]

    def wait_scatter():
        pltpu.make_async_copy(res_ref, out_hbm.at[pl.ds(0, rows), :], scatter_sem).wait()

    @pl.when(kt == 0)
    def _():
        @pl.when((e > 0) | (rb > 0))
        def _():
            wait_scatter()

    seg = rows // nkt
    for u in range(seg):
        r = kt * seg + u
        pltpu.make_async_copy(out_hbm.at[pl.ds(idx_ref[base + r], 1), :], res_ref.at[pl.ds(r, 1), :],
                              gather_sem).start()

    def partial_product():
        return _dot(hid_ref[...], wd_ref[...].astype(BF16))

    @pl.when(kt == 0)
    def _():
        acc_ref[...] = partial_product()

    @pl.when((kt > 0) & (kt < last_kt))
    def _():
        acc_ref[...] += partial_product()

    @pl.when(kt == last_kt)
    def _():
        pltpu.make_async_copy(out_hbm.at[pl.ds(0, rows), :], res_ref, gather_sem).wait()
        res_ref[...] += acc_ref[...] + partial_product()

        def start(r, carry):
            pltpu.make_async_copy(res_ref.at[pl.ds(r, 1), :], out_hbm.at[pl.ds(idx_ref[base + r], 1), :],
                                  scatter_sem).start()
            return carry
        lax.fori_loop(0, rows, start, 0, unroll=DMA_UNROLL)

        @pl.when((e == pl.num_programs(0) - 1) & (rb == pl.num_programs(1) - 1))
        def _():
            wait_scatter()


def _moe_down(idx_flat, hid, w_down, x2, cap):
    n = x2.shape[0]
    rows = min(R_DOWN, cap)
    tk = TK_DOWN
    assert EXPERT_FF // tk >= 2
    grid_spec = pltpu.PrefetchScalarGridSpec(
        num_scalar_prefetch=1,
        grid=(N_EXPERTS, cap // rows, EXPERT_FF // tk),
        in_specs=[
            pl.BlockSpec((None, rows, tk), lambda e, rb, kt, idx: (e, rb, kt)),
            pl.BlockSpec((None, tk, D_MODEL), lambda e, rb, kt, idx: (e, kt, 0)),
            pl.BlockSpec(memory_space=pl.ANY),
        ],
        out_specs=pl.BlockSpec(memory_space=pl.ANY),
        scratch_shapes=[
            pltpu.VMEM((rows, D_MODEL), F32),
            pltpu.VMEM((rows, D_MODEL), F32),
            pltpu.SemaphoreType.DMA((2,)),
        ],
    )
    return pl.pallas_call(
        functools.partial(_moe_down_body, cap=cap, rows=rows, nkt=EXPERT_FF // tk),
        grid_spec=grid_spec,
        out_shape=jax.ShapeDtypeStruct((n, D_MODEL), F32),
        input_output_aliases={3: 0},
        compiler_params=_cparams(("arbitrary", "arbitrary", "arbitrary")),
    )(idx_flat, hid, w_down, x2)


def _pad_rope_cols(t):
    half = QK_ROPE // 2
    z = jnp.zeros(t.shape[:-1] + (half,), t.dtype)
    return jnp.concatenate([t[..., :half], z, t[..., half:], z], axis=-1)


def _rope_tables_padded(seq):
    inv = ROPE_THETA ** (-jnp.arange(0, QK_ROPE, 2, dtype=F32) / QK_ROPE)
    ang = jnp.arange(seq, dtype=F32)[:, None] * inv[None, :]
    cos, sin = jnp.cos(ang), jnp.sin(ang)
    z = jnp.zeros_like(cos)
    return jnp.concatenate([cos, z, cos, z], axis=-1), jnp.concatenate([-sin, z, sin, z], axis=-1)


def _prepare_weights(w_in, mla_w_q_b, mla_w_kv_b, w_o, w_router, mla_q_rope_norm_g, mla_k_rope_norm_g):
    qkv_cols = 3 * A_WIDTH
    w_a = w_in[:, :qkv_cols].astype(BF16)
    w_ql = w_in[:, qkv_cols:qkv_cols + Q_LORA].astype(BF16)
    w_kvl = w_in[:, qkv_cols + Q_LORA:qkv_cols + Q_LORA + KV_LORA].astype(BF16)
    w_kr = _pad_rope_cols(w_in[:, qkv_cols + Q_LORA + KV_LORA:]).astype(BF16)
    wq = mla_w_q_b.reshape(Q_LORA, MLA_HEADS, QK_NOPE + QK_ROPE)
    wq_pad = jnp.concatenate([wq[..., :QK_NOPE], _pad_rope_cols(wq[..., QK_NOPE:])], axis=-1)
    wq_pad = wq_pad.reshape(Q_LORA, MLA_HEADS * MLA_HEAD_PAD).astype(BF16)
    wr = jnp.pad(w_router, ((0, 0), (0, LANES - N_EXPERTS)))
    wr1 = wr.astype(BF16)
    wr2 = (wr - wr1.astype(F32)).astype(BF16)
    return dict(
        w_a=w_a, w_ql=w_ql, w_kvl=w_kvl, w_kr=w_kr, wq_pad=wq_pad,
        wkv=mla_w_kv_b.astype(BF16), w_o=w_o.astype(BF16), wr1=wr1, wr2=wr2,
        gqr_pad=_pad_rope_cols(mla_q_rope_norm_g)[None, :], gkr_pad=_pad_rope_cols(mla_k_rope_norm_g)[None, :],
    )


def _encoder_layer(x3, pw, norm_mix_g, a_q_norm_g, a_k_norm_g, mla_q_a_norm_g, mla_kv_a_norm_g,
                   mla_q_nope_norm_g, mla_k_nope_norm_g, out_norm_a_g, out_norm_b_g, norm_ffn_g,
                   w_gate, w_up, w_down):
    b, s, d = x3.shape
    n = b * s
    x = x3.reshape(n, d)
    row = lambda v: v[None, :]
    cos_pad, sin_pad = _rope_tables_padded(s)

    qkv_by_dilation = _inproj_a(x, row(norm_mix_g), pw["w_a"], row(a_q_norm_g), row(a_k_norm_g), b, s)
    branches = [_dilated_branch(qkv) for qkv in qkv_by_dilation]

    cq, ckv, kr = _inproj_lat(x, row(norm_mix_g), pw["w_ql"], pw["w_kvl"], pw["w_kr"], row(mla_q_a_norm_g),
                              row(mla_kv_a_norm_g), pw["gkr_pad"], cos_pad, sin_pad, s)
    qm, km, vm = _mla_up(cq, ckv, kr, pw["wq_pad"], pw["wkv"], row(mla_q_nope_norm_g), pw["gqr_pad"],
                         row(mla_k_nope_norm_g), cos_pad, sin_pad, s)
    wide = MLA_HEADS * MLA_HEAD_PAD
    ob = _mla_flash(qm.reshape(b, s, wide), km.reshape(b, s, wide), vm.reshape(b, s, wide))

    x2 = _outproj(branches, ob.reshape(n, MLA_WIDTH), row(out_norm_a_g), row(out_norm_b_g), pw["w_o"], x, s)

    h2, aff_t = _router(x2, row(norm_ffn_g), pw["wr1"], pw["wr2"])
    cap = CAPACITY_FACTOR * n // N_EXPERTS
    idx, gate = _topk(aff_t, cap)
    idx_flat = idx.reshape(N_EXPERTS * cap)
    hid = _moe_up(idx_flat, h2, gate, w_gate, w_up, cap)
    y = _moe_down(idx_flat, hid, w_down, x2, cap)
    return y.reshape(b, s, d)


def kernel(x_prompt, x_sample, norm_mix_g, w_in, a_q_norm_g, a_k_norm_g, mla_q_a_norm_g, mla_w_q_b, mla_kv_a_norm_g, mla_w_kv_b, mla_q_nope_norm_g, mla_q_rope_norm_g, mla_k_nope_norm_g, mla_k_rope_norm_g, out_norm_a_g, out_norm_b_g, w_o, norm_ffn_g, w_router, w_gate, w_up, w_down):
    depth = w_in.shape[0]
    prepared = [_prepare_weights(w_in[l], mla_w_q_b[l], mla_w_kv_b[l], w_o[l], w_router[l],
                                 mla_q_rope_norm_g[l], mla_k_rope_norm_g[l]) for l in range(depth)]

    def run_trunk(x):
        for l in range(depth):
            pw = prepared[l]
            x = _encoder_layer(x, pw, norm_mix_g[l], a_q_norm_g[l], a_k_norm_g[l], mla_q_a_norm_g[l],
                               mla_kv_a_norm_g[l], mla_q_nope_norm_g[l], mla_k_nope_norm_g[l],
                               out_norm_a_g[l], out_norm_b_g[l], norm_ffn_g[l], w_gate[l], w_up[l], w_down[l])
        return x

    return run_trunk(x_prompt), run_trunk(x_sample)
```

```python
import functools
import math

import jax
import jax.numpy as jnp
from jax import lax
from jax.experimental import pallas as pl
from jax.experimental.pallas import tpu as pltpu

F32 = jnp.float32
BF16 = jnp.bfloat16

D_MODEL = 4096
A_HEADS = 16
A_HEAD_DIM = 128
A_WIDTH = A_HEADS * A_HEAD_DIM
DILATED_PATTERNS = ((128, 1), (512, 4), (2048, 16))
DILATIONS = tuple(d for _, d in DILATED_PATTERNS)
MLA_HEADS = 16
QK_NOPE = 128
QK_ROPE = 64
V_HEAD = 128
Q_LORA = 1024
KV_LORA = 512
MLA_WIDTH = MLA_HEADS * V_HEAD
ROPE_THETA = 10000.0
N_EXPERTS = 16
EXPERT_FF = 2048
CAPACITY_FACTOR = 2
EPS = 1e-6

LANES = 128
MLA_HEAD_PAD = 2 * LANES
N_SIDE = 64

TM_INPROJ = 512
TN_INPROJ = 512
TM_LAT = 256
TM_MLAUP = 256
TQ_MLA = 1024
TK_MLA = 2048
TQ_DIL = 128
TB_DIL = 512
TM_OUT = 512
TN_OUT = 512
TM_ROUTER = 256
TF_MOE = 256
ROW_CHUNK_UP = 512
TK_DOWN = 256
R_DOWN = 1024
G_GATHER = 256
DMA_UNROLL = 8
VMEM_LIMIT = 56 * 1024 * 1024


def _cparams(sem):
    return pltpu.CompilerParams(dimension_semantics=sem, vmem_limit_bytes=VMEM_LIMIT)


def _rms(x, g):
    ms = jnp.mean(x * x, axis=-1, keepdims=True)
    return x * lax.rsqrt(ms + EPS) * g


def _dot(a, b):
    return jnp.dot(a, b, preferred_element_type=F32)


def _dot_nt(a, b):
    return lax.dot_general(a, b, (((1,), (1,)), ((), ())), preferred_element_type=F32)


def _inproj_a_body(x_ref, g_ref, w_ref, gq_ref, gk_ref, *refs, chunk, ncol_tiles):
    nd = len(DILATIONS)
    out_refs = refs[:nd]
    xn_ref, y0_ref, y1_ref, z_ref = refs[nd:]
    t = pl.program_id(0)
    j = t % ncol_tiles
    jp = (t + ncol_tiles - 1) % ncol_tiles
    tm = x_ref.shape[0]
    tn = w_ref.shape[1]
    nq = A_WIDTH // tn
    planes = tn // LANES

    @pl.when(t == 0)
    def _():
        y1_ref[...] = jnp.zeros(y1_ref.shape, F32)

    @pl.when(j == 0)
    def _():
        def norm_chunk(c, carry):
            r = pl.multiple_of(c * chunk, chunk)
            xn_ref[pl.ds(r, chunk), :] = _rms(x_ref[pl.ds(r, chunk), :], g_ref[...]).astype(BF16)
            return carry

        lax.fori_loop(0, tm // chunk, norm_chunk, 0)

    is_q = jp < nq
    is_k = (jp >= nq) & (jp < 2 * nq)
    q_gain = gq_ref[...] * (A_HEAD_DIM ** -0.5 * math.log2(math.e))
    gain = jnp.where(is_q, q_gain, jnp.where(is_k, gk_ref[...], jnp.ones_like(q_gain)))

    def work(y_cur, y_prev):
        y_cur[...] = _dot(xn_ref[...], w_ref[...])
        for c in range(planes):
            yh = y_prev[:, c * LANES:(c + 1) * LANES]
            ms = jnp.mean(yh * yh, axis=-1, keepdims=True)
            factor = jnp.where(is_q | is_k, lax.rsqrt(ms + EPS), jnp.ones_like(ms))
            z_ref[c] = yh * factor * gain
        for o_ref, d in zip(out_refs, DILATIONS):
            for c in range(planes):
                cols = slice(c * LANES, (c + 1) * LANES)
                if d == 1:
                    o_ref[:, cols] = z_ref[c].astype(BF16)
                else:
                    for r in range(d):
                        o_ref[r, :, cols] = z_ref.at[c][pl.ds(r, tm // d, stride=d), :].astype(BF16)

    @pl.when(t % 2 == 0)
    def _():
        work(y0_ref, y1_ref)

    @pl.when(t % 2 == 1)
    def _():
        work(y1_ref, y0_ref)


def _inproj_a(x, g, w_a, gq, gk, batch, seq):
    n = x.shape[0]
    tm = min(TM_INPROJ, seq)
    tn = TN_INPROJ
    ncol = w_a.shape[1]
    nseq = seq // tm
    nrow_tiles = n // tm
    nct = ncol // tn
    steps = nrow_tiles * nct + 1

    def prev_tile(t):
        tp = jnp.maximum(t - 1, 0)
        return tp // nct, tp % nct

    def residue_major(t):
        i, j = prev_tile(t)
        return i // nseq, 0, i % nseq, j

    out_specs, out_shape = [], []
    for d in DILATIONS:
        if d == 1:
            out_specs.append(pl.BlockSpec((tm, tn), prev_tile))
            out_shape.append(jax.ShapeDtypeStruct((n, ncol), BF16))
        else:
            out_specs.append(pl.BlockSpec((None, d, tm // d, tn), residue_major))
            out_shape.append(jax.ShapeDtypeStruct((batch, d, seq // d, ncol), BF16))
    outs = pl.pallas_call(
        functools.partial(_inproj_a_body, chunk=128, ncol_tiles=nct),
        grid=(steps,),
        in_specs=[
            pl.BlockSpec((tm, D_MODEL), lambda t: (jnp.minimum(t // nct, nrow_tiles - 1), 0)),
            pl.BlockSpec((1, D_MODEL), lambda t: (0, 0)),
            pl.BlockSpec((D_MODEL, tn), lambda t: (0, t % nct)),
            pl.BlockSpec((1, A_HEAD_DIM), lambda t: (0, 0)),
            pl.BlockSpec((1, A_HEAD_DIM), lambda t: (0, 0)),
        ],
        out_specs=out_specs,
        out_shape=out_shape,
        scratch_shapes=[
            pltpu.VMEM((tm, D_MODEL), BF16),
            pltpu.VMEM((tm, tn), F32),
            pltpu.VMEM((tm, tn), F32),
            pltpu.VMEM((tn // LANES, tm, LANES), F32),
        ],
        compiler_params=_cparams(("arbitrary",)),
    )(x, g, w_a, gq, gk)
    return [o.reshape(batch, d, seq // d, ncol) for o, d in zip(outs, DILATIONS)]


def _rope_padded(t, cos, sin):
    return t * cos + pltpu.roll(t, 2 * (QK_ROPE // 2), 1) * sin


def _inproj_lat_body(x_ref, g_ref, wq_ref, wkv_ref, wkr_ref, gq_ref, gkv_ref, gkr_ref, cos_ref, sin_ref,
                     cq_ref, ckv_ref, kr_ref):
    xn = _rms(x_ref[...], g_ref[...]).astype(BF16)
    cq_ref[...] = _rms(_dot(xn, wq_ref[...]), gq_ref[...]).astype(BF16)
    ckv_ref[...] = _rms(_dot(xn, wkv_ref[...]), gkv_ref[...]).astype(BF16)
    kr = _dot(xn, wkr_ref[...])
    ms = jnp.sum(kr * kr, axis=-1, keepdims=True) * (1.0 / QK_ROPE)
    krn = kr * lax.rsqrt(ms + EPS) * gkr_ref[...]
    kr_ref[...] = _rope_padded(krn, cos_ref[...], sin_ref[...]).astype(BF16)


def _inproj_lat(x, g, wq, wkv, wkr, gq, gkv, gkr, cos_pad, sin_pad, seq):
    n = x.shape[0]
    tm = min(TM_LAT, seq)
    nseq = seq // tm
    const = lambda i: (0, 0)
    return pl.pallas_call(
        _inproj_lat_body,
        grid=(n // tm,),
        in_specs=[
            pl.BlockSpec((tm, D_MODEL), lambda i: (i, 0)),
            pl.BlockSpec((1, D_MODEL), const),
            pl.BlockSpec((D_MODEL, Q_LORA), const),
            pl.BlockSpec((D_MODEL, KV_LORA), const),
            pl.BlockSpec((D_MODEL, LANES), const),
            pl.BlockSpec((1, Q_LORA), const),
            pl.BlockSpec((1, KV_LORA), const),
            pl.BlockSpec((1, LANES), const),
            pl.BlockSpec((tm, LANES), lambda i: (i % nseq, 0)),
            pl.BlockSpec((tm, LANES), lambda i: (i % nseq, 0)),
        ],
        out_specs=[
            pl.BlockSpec((tm, Q_LORA), lambda i: (i, 0)),
            pl.BlockSpec((tm, KV_LORA), lambda i: (i, 0)),
            pl.BlockSpec((tm, LANES), lambda i: (i, 0)),
        ],
        out_shape=[
            jax.ShapeDtypeStruct((n, Q_LORA), BF16),
            jax.ShapeDtypeStruct((n, KV_LORA), BF16),
            jax.ShapeDtypeStruct((n, LANES), BF16),
        ],
        compiler_params=_cparams(("parallel",)),
    )(x, g, wq, wkv, wkr, gq, gkv, gkr, cos_pad, sin_pad)


def _mla_up_body(cq_ref, ckv_ref, kr_ref, wq_ref, wkv_ref, gqn_ref, gqr_ref, gkn_ref, cos_ref, sin_ref,
                 q_ref, k_ref, v_ref):
    cq = cq_ref[...]
    ckv = ckv_ref[...]
    kr = kr_ref[...]
    cos = cos_ref[...]
    sin = sin_ref[...]
    scale = (QK_NOPE + QK_ROPE) ** -0.5 * math.log2(math.e)
    ones = jnp.ones((cq.shape[0], V_HEAD), BF16)
    for h in range(MLA_HEADS):
        lo = h * MLA_HEAD_PAD
        q = _dot(cq, wq_ref[:, lo:lo + MLA_HEAD_PAD])
        qn = _rms(q[:, :LANES], gqn_ref[...] * scale)
        qr = q[:, LANES:]
        ms = jnp.sum(qr * qr, axis=-1, keepdims=True) * (1.0 / QK_ROPE)
        qr = _rope_padded(qr * lax.rsqrt(ms + EPS) * (gqr_ref[...] * scale), cos, sin)
        q_ref[:, lo:lo + LANES] = qn.astype(BF16)
        q_ref[:, lo + LANES:lo + MLA_HEAD_PAD] = qr.astype(BF16)
        kv = _dot(ckv, wkv_ref[:, lo:lo + MLA_HEAD_PAD])
        k_ref[:, lo:lo + LANES] = _rms(kv[:, :LANES], gkn_ref[...]).astype(BF16)
        k_ref[:, lo + LANES:lo + MLA_HEAD_PAD] = kr
        v_ref[:, lo:lo + V_HEAD] = kv[:, LANES:].astype(BF16)
        v_ref[:, lo + V_HEAD:lo + 2 * V_HEAD] = ones


def _mla_up(cq, ckv, kr, wq_pad, wkv, gqn, gqr_pad, gkn, cos_pad, sin_pad, seq):
    n = cq.shape[0]
    tm = min(TM_MLAUP, seq)
    nseq = seq // tm
    const = lambda i: (0, 0)
    row = lambda i: (i, 0)
    wide = MLA_HEADS * MLA_HEAD_PAD
    return pl.pallas_call(
        _mla_up_body,
        grid=(n // tm,),
        in_specs=[
            pl.BlockSpec((tm, Q_LORA), row),
            pl.BlockSpec((tm, KV_LORA), row),
            pl.BlockSpec((tm, LANES), row),
            pl.BlockSpec((Q_LORA, wide), const),
            pl.BlockSpec((KV_LORA, wide), const),
            pl.BlockSpec((1, LANES), const),
            pl.BlockSpec((1, LANES), const),
            pl.BlockSpec((1, LANES), const),
            pl.BlockSpec((tm, LANES), lambda i: (i % nseq, 0)),
            pl.BlockSpec((tm, LANES), lambda i: (i % nseq, 0)),
        ],
        out_specs=[
            pl.BlockSpec((tm, wide), row),
            pl.BlockSpec((tm, wide), row),
            pl.BlockSpec((tm, wide), row),
        ],
        out_shape=[
            jax.ShapeDtypeStruct((n, wide), BF16),
            jax.ShapeDtypeStruct((n, wide), BF16),
            jax.ShapeDtypeStruct((n, wide), BF16),
        ],
        compiler_params=_cparams(("parallel",)),
    )(cq, ckv, kr, wq_pad, wkv, gqn, gqr_pad, gkn, cos_pad, sin_pad)


def _mla_flash_body(q_ref, k_ref, v_ref, o_ref, m_ref, acc_ref, s0_ref, s1_ref, *, tk):
    nk = k_ref.shape[0] // tk
    m_ref[...] = jnp.full(m_ref.shape, -jnp.inf, F32)
    acc_ref[...] = jnp.zeros(acc_ref.shape, F32)

    def scores(c, s_ref):
        r = pl.multiple_of(c * tk, tk)
        s_ref[...] = _dot_nt(q_ref[...], k_ref[pl.ds(r, tk), :])

    def update(c, s_ref):
        r = pl.multiple_of(c * tk, tk)
        s = s_ref[...]
        m_old = m_ref[...]
        m_new = jnp.maximum(m_old, jnp.max(s, axis=-1, keepdims=True))
        alpha = jnp.exp2(m_old - m_new)
        p = jnp.exp2((s - jnp.tile(m_new, (1, tk // LANES))).astype(BF16))
        acc_ref[...] = jnp.tile(alpha, (1, 2)) * acc_ref[...] + _dot(p, v_ref[pl.ds(r, tk), :])
        m_ref[...] = m_new

    scores(0, s0_ref)

    def pair(jj, carry):
        c = 2 * jj
        scores(c + 1, s1_ref)
        update(c, s0_ref)
        scores(c + 2, s0_ref)
        update(c + 1, s1_ref)
        return carry

    lax.fori_loop(0, nk // 2 - 1, pair, 0)
    scores(nk - 1, s1_ref)
    update(nk - 2, s0_ref)
    update(nk - 1, s1_ref)
    o_ref[...] = (acc_ref[:, :V_HEAD] / acc_ref[:, V_HEAD:]).astype(o_ref.dtype)


def _mla_flash(qm, km, vm):
    b, s, _ = qm.shape
    tq = min(TQ_MLA, s)
    tk = min(TK_MLA, s // 2)
    return pl.pallas_call(
        functools.partial(_mla_flash_body, tk=tk),
        grid=(b, MLA_HEADS, s // tq),
        in_specs=[
            pl.BlockSpec((None, tq, MLA_HEAD_PAD), lambda bi, h, i: (bi, i, h)),
            pl.BlockSpec((None, s, MLA_HEAD_PAD), lambda bi, h, i: (bi, 0, h)),
            pl.BlockSpec((None, s, 2 * V_HEAD), lambda bi, h, i: (bi, 0, h)),
        ],
        out_specs=pl.BlockSpec((None, tq, V_HEAD), lambda bi, h, i: (bi, i, h)),
        out_shape=jax.ShapeDtypeStruct((b, s, MLA_WIDTH), BF16),
        scratch_shapes=[pltpu.VMEM((tq, LANES), F32), pltpu.VMEM((tq, 2 * V_HEAD), F32),
                        pltpu.VMEM((tq, tk), F32), pltpu.VMEM((tq, tk), F32)],
        compiler_params=_cparams(("parallel", "parallel", "arbitrary")),
    )(qm, km, vm)


def _dilated_body(q_ref, k0_ref, kp_ref, kn_ref, v0_ref, vp_ref, vn_ref, o_ref, lse_ref, kw_ref, vw_ref,
                  s_ref, *, dil, nq, uq):
    i = pl.program_id(2)
    tb = q_ref.shape[0]
    units = tb // uq
    kw_ref[0:N_SIDE, :] = kp_ref[...]
    kw_ref[N_SIDE:N_SIDE + tb, :] = k0_ref[...]
    kw_ref[N_SIDE + tb:, :] = kn_ref[...]
    vw_ref[0:N_SIDE, :] = vp_ref[...]
    vw_ref[N_SIDE:N_SIDE + tb, :] = v0_ref[...]
    vw_ref[N_SIDE + tb:, :] = vn_ref[...]

    @pl.when(i == nq - 1)
    def _():
        kw_ref[N_SIDE + tb:, :] = jnp.zeros((N_SIDE, A_WIDTH), BF16)
        vw_ref[N_SIDE + tb:, :] = jnp.zeros((N_SIDE, A_WIDTH), BF16)

    win = uq + 2 * N_SIDE
    row = lax.broadcasted_iota(jnp.int32, (uq, win), 0)
    col = lax.broadcasted_iota(jnp.int32, (uq, win), 1)
    rel = col - N_SIDE - row
    arel = jnp.abs(rel)
    near = arel <= N_SIDE
    token_dist = (dil * arel).astype(F32)
    lane = lax.broadcasted_iota(jnp.int32, (uq, LANES), 1)
    ones = jnp.ones((win, LANES), BF16)

    for u in range(units):
        valid = near
        if u == 0:
            valid = valid & ((rel >= 0) | (row >= N_SIDE) | (i > 0))
        if u == units - 1:
            valid = valid & ((rel < N_SIDE) | (row < uq - N_SIDE) | (i < nq - 1))
        dist = jnp.where(valid, token_dist, jnp.inf)
        qrows = slice(u * uq, (u + 1) * uq)
        wrows = slice(u * uq, u * uq + win)
        sbuf = s_ref.at[u % 2]
        for h in range(A_HEADS):
            sl = slice(h * A_HEAD_DIM, (h + 1) * A_HEAD_DIM)
            sbuf[h] = _dot_nt(q_ref[qrows, sl], kw_ref[wrows, sl])
        lse_all = jnp.zeros((uq, LANES), F32)
        for h in range(A_HEADS):
            sl = slice(h * A_HEAD_DIM, (h + 1) * A_HEAD_DIM)
            slope = 2.0 ** (-8.0 * (h + 1) / A_HEADS) * math.log2(math.e)
            s = sbuf[h] - slope * dist
            m = jnp.max(s, axis=-1, keepdims=True)
            p = jnp.exp2((s - m).astype(BF16))
            l = _dot(p, ones)
            o_ref[qrows, sl] = (_dot(p, vw_ref[wrows, sl]) / l).astype(o_ref.dtype)
            lse_all = jnp.where(lane == h, m + jnp.log2(l), lse_all)
        lse_ref[qrows, :] = lse_all


def _dilated_branch(qkv):
    b, dil, sub, _ = qkv.shape
    uq = TQ_DIL
    tq = min(TB_DIL, sub)
    nq = sub // tq
    nhalo = sub // N_SIDE
    per = tq // N_SIDE

    def own(which):
        return pl.BlockSpec((None, None, tq, A_WIDTH), lambda bi, r, i: (bi, r, i, which))

    def prev(which):
        return pl.BlockSpec((None, None, N_SIDE, A_WIDTH),
                            lambda bi, r, i: (bi, r, jnp.maximum(per * i - 1, 0), which))

    def nxt(which):
        return pl.BlockSpec((None, None, N_SIDE, A_WIDTH),
                            lambda bi, r, i: (bi, r, jnp.minimum(per * (i + 1), nhalo - 1), which))

    return pl.pallas_call(
        functools.partial(_dilated_body, dil=dil, nq=nq, uq=uq),
        grid=(b, dil, nq),
        in_specs=[own(0), own(1), prev(1), nxt(1), own(2), prev(2), nxt(2)],
        out_specs=[
            pl.BlockSpec((None, None, tq, A_WIDTH), lambda bi, r, i: (bi, r, i, 0)),
            pl.BlockSpec((None, None, tq, LANES), lambda bi, r, i: (bi, r, i, 0)),
        ],
        out_shape=[
            jax.ShapeDtypeStruct((b, dil, sub, A_WIDTH), BF16),
            jax.ShapeDtypeStruct((b, dil, sub, LANES), F32),
        ],
        scratch_shapes=[
            pltpu.VMEM((tq + 2 * N_SIDE, A_WIDTH), BF16),
            pltpu.VMEM((tq + 2 * N_SIDE, A_WIDTH), BF16),
            pltpu.VMEM((2, A_HEADS, uq, uq + 2 * N_SIDE), F32),
        ],
        compiler_params=_cparams(("parallel", "parallel", "arbitrary")),
    )(qkv, qkv, qkv, qkv, qkv, qkv, qkv)


def _outproj_body(*refs):
    nd = len(DILATIONS)
    o_refs, l_refs = refs[:nd], refs[nd:2 * nd]
    ob_ref, ga_ref, gb_ref, w_ref, x_ref, y_ref, mix_ref, oa_ref, t_ref, lt_ref = refs[2 * nd:]
    j = pl.program_id(1)
    tm = x_ref.shape[0]

    def rows_of(d, r):
        return slice(None) if d == 1 else pl.ds(r, tm // d, stride=d)

    @pl.when(j == 0)
    def _():
        for i, d in enumerate(DILATIONS):
            for r in range(d):
                lt_ref.at[i][rows_of(d, r), :] = l_refs[i][r]
        ls = [lt_ref[i] for i in range(nd)]
        mx = functools.reduce(jnp.maximum, ls)
        es = [jnp.exp2(l - mx) for l in ls]
        inv = 1.0 / functools.reduce(lambda a, b: a + b, es)
        head_of_col = lax.broadcasted_iota(jnp.int32, (2 * LANES, A_WIDTH), 1) // A_HEAD_DIM
        lane_of_row = lax.broadcasted_iota(jnp.int32, (2 * LANES, A_WIDTH), 0) % LANES
        expand = (head_of_col == lane_of_row).astype(BF16)
        for i, d in enumerate(DILATIONS):
            w = es[i] * inv
            hi = w.astype(BF16)
            lo = (w - hi.astype(F32)).astype(BF16)
            wexp = _dot(jnp.concatenate([hi, lo], axis=1), expand)
            for h in range(A_HEADS):
                sl = slice(h * A_HEAD_DIM, (h + 1) * A_HEAD_DIM)
                for r in range(d):
                    t_ref.at[h][rows_of(d, r), :] = o_refs[i][r, :, sl].astype(F32)
                if i == 0:
                    oa_ref[h] = wexp[:, sl] * t_ref[h]
                else:
                    oa_ref[h] += wexp[:, sl] * t_ref[h]
        ss = jnp.zeros((tm, 1), F32)
        for h in range(A_HEADS):
            oa = oa_ref[h]
            ss = ss + jnp.sum(oa * oa, axis=-1, keepdims=True)
        inv_a = lax.rsqrt(ss * (1.0 / A_WIDTH) + EPS)
        for h in range(A_HEADS):
            sl = slice(h * A_HEAD_DIM, (h + 1) * A_HEAD_DIM)
            mix_ref[:, sl] = (oa_ref[h] * inv_a * ga_ref[:, sl]).astype(BF16)
        mix_ref[:, A_WIDTH:] = _rms(ob_ref[...].astype(F32), gb_ref[...]).astype(BF16)

    y_ref[...] = x_ref[...] + _dot(mix_ref[...], w_ref[...])


def _outproj(branches, ob, ga, gb, w_o, x, seq):
    n = x.shape[0]
    tm = min(TM_OUT, seq)
    tn = TN_OUT
    nseq = seq // tm
    row = lambda i, j: (i, 0)
    const = lambda i, j: (0, 0)
    mixw = A_WIDTH + MLA_WIDTH
    res = lambda i, j: (i // nseq, 0, i % nseq, 0)
    o_specs = [pl.BlockSpec((None, d, tm // d, A_WIDTH), res) for d in DILATIONS]
    l_specs = [pl.BlockSpec((None, d, tm // d, LANES), res) for d in DILATIONS]
    return pl.pallas_call(
        _outproj_body,
        grid=(n // tm, D_MODEL // tn),
        in_specs=o_specs + l_specs + [
            pl.BlockSpec((tm, MLA_WIDTH), row),
            pl.BlockSpec((1, A_WIDTH), const), pl.BlockSpec((1, MLA_WIDTH), const),
            pl.BlockSpec((mixw, tn), lambda i, j: (0, j)),
            pl.BlockSpec((tm, tn), lambda i, j: (i, j)),
        ],
        out_specs=pl.BlockSpec((tm, tn), lambda i, j: (i, j)),
        out_shape=jax.ShapeDtypeStruct((n, D_MODEL), F32),
        scratch_shapes=[
            pltpu.VMEM((tm, mixw), BF16),
            pltpu.VMEM((A_HEADS, tm, A_HEAD_DIM), F32),
            pltpu.VMEM((A_HEADS, tm, A_HEAD_DIM), F32),
            pltpu.VMEM((len(DILATIONS), tm, LANES), F32),
        ],
        compiler_params=_cparams(("parallel", "arbitrary")),
    )(*[o for o, _ in branches], *[l for _, l in branches], ob, ga, gb, w_o, x)


def _router_body(x_ref, g_ref, w1_ref, w2_ref, h_ref, aff_ref):
    h = _rms(x_ref[...], g_ref[...])
    half = D_MODEL // 2
    lo = pltpu.bitcast(h[:, :half].astype(BF16).astype(F32), jnp.uint32)
    hi = pltpu.bitcast(h[:, half:].astype(BF16).astype(F32), jnp.uint32)
    h_ref[...] = hi | (lo >> 16)
    h1 = h.astype(BF16)
    h2 = (h - h1.astype(F32)).astype(BF16)
    logits = _dot(h1, w1_ref[...]) + _dot(h1, w2_ref[...]) + _dot(h2, w1_ref[...])
    lane = lax.broadcasted_iota(jnp.int32, logits.shape, 1)
    logits = jnp.where(lane < N_EXPERTS, logits, -jnp.inf)
    m = jnp.max(logits, axis=-1, keepdims=True)
    e = jnp.exp(logits - m)
    aff_ref[...] = e / jnp.sum(e, axis=-1, keepdims=True)


def _router(x2, g, wr1, wr2):
    n = x2.shape[0]
    tm = min(TM_ROUTER, n)
    h2, aff = pl.pallas_call(
        _router_body,
        grid=(n // tm,),
        in_specs=[
            pl.BlockSpec((tm, D_MODEL), lambda i: (i, 0)),
            pl.BlockSpec((1, D_MODEL), lambda i: (0, 0)),
            pl.BlockSpec((D_MODEL, LANES), lambda i: (0, 0)),
            pl.BlockSpec((D_MODEL, LANES), lambda i: (0, 0)),
        ],
        out_specs=[
            pl.BlockSpec((tm, D_MODEL // 2), lambda i: (i, 0)),
            pl.BlockSpec((tm, LANES), lambda i: (i, 0)),
        ],
        out_shape=[
            jax.ShapeDtypeStruct((n, D_MODEL // 2), jnp.uint32),
            jax.ShapeDtypeStruct((n, LANES), F32),
        ],
        compiler_params=_cparams(("parallel",)),
    )(x2, g, wr1, wr2)
    return h2, aff[:, :N_EXPERTS].T


def _sum_all(x):
    return jnp.sum(jnp.sum(x, axis=1, keepdims=True), axis=0, keepdims=True)


def _topk_body(a_ref, idx_ref, gate_ref, *, cap):
    a = a_ref[...]
    nc = a.shape[0]
    bits = pltpu.bitcast(a, jnp.int32)
    thr = jnp.zeros((1, 1), jnp.int32)
    for bit in range(30, -1, -1):
        cand = thr + (1 << bit)
        cnt = _sum_all((bits >= cand).astype(F32))
        thr = jnp.where(cnt >= cap, cand, thr)
    gt = bits > thr
    eq = bits == thr
    eqf = eq.astype(F32)
    need = cap - _sum_all(gt.astype(F32))

    li = lax.broadcasted_iota(jnp.int32, (LANES, LANES), 0)
    lj = lax.broadcasted_iota(jnp.int32, (LANES, LANES), 1)
    upper_incl = (li <= lj).astype(BF16)
    ci = lax.broadcasted_iota(jnp.int32, (nc, nc), 0)
    cj = lax.broadcasted_iota(jnp.int32, (nc, nc), 1)
    lower_strict = (cj < ci).astype(BF16)
    upper_incl_c = (ci <= cj).astype(BF16)

    p_eq = _dot(eqf.astype(BF16), upper_incl)
    tot_eq = jnp.broadcast_to(p_eq[:, LANES - 1:LANES], (nc, LANES))
    off_eq = _dot(lower_strict, tot_eq.astype(BF16))
    sel = gt | (eq & ((p_eq - eqf + off_eq) < need))
    selb = sel.astype(F32).astype(BF16)

    p_sel = _dot(selb, upper_incl)
    tot_row = _dot_nt(jnp.ones((8, LANES), BF16), selb)
    offi_row = _dot(tot_row.astype(BF16), upper_incl_c)[0:1, :]
    offe_row = offi_row - tot_row[0:1, :]

    slot_c = lax.broadcasted_iota(jnp.int32, (cap, nc), 0).astype(F32)
    chunk_of = jnp.sum((offi_row <= slot_c).astype(F32), axis=1, keepdims=True)
    onehot = lax.broadcasted_iota(jnp.int32, (cap, nc), 1).astype(F32) == chunk_of
    onehot_b = onehot.astype(F32).astype(BF16)
    run = _dot(onehot_b, p_sel.astype(BF16))
    offe = jnp.sum(jnp.where(onehot, offe_row, 0.0), axis=1, keepdims=True)
    slot_l = lax.broadcasted_iota(jnp.int32, (cap, LANES), 0).astype(F32)
    lane_of = jnp.sum(((run + offe) <= slot_l).astype(F32), axis=1, keepdims=True)
    idx_ref[...] = (chunk_of * LANES + lane_of).astype(jnp.int32)

    a1 = a.astype(BF16)
    r1 = a - a1.astype(F32)
    a2 = r1.astype(BF16)
    a3 = (r1 - a2.astype(F32)).astype(BF16)
    rows = _dot(onehot_b, a1) + _dot(onehot_b, a2) + _dot(onehot_b, a3)
    lanes = lax.broadcasted_iota(jnp.int32, (cap, LANES), 1).astype(F32)
    gate_ref[...] = jnp.sum(jnp.where(lanes == lane_of, rows, 0.0), axis=1, keepdims=True)


def _topk(aff_t, cap):
    e, n = aff_t.shape
    nc = n // LANES
    return pl.pallas_call(
        functools.partial(_topk_body, cap=cap),
        grid=(e,),
        in_specs=[pl.BlockSpec((None, nc, LANES), lambda i: (i, 0, 0))],
        out_specs=[
            pl.BlockSpec((None, cap, 1), lambda i: (i, 0, 0)),
            pl.BlockSpec((None, cap, 1), lambda i: (i, 0, 0)),
        ],
        out_shape=[
            jax.ShapeDtypeStruct((e, cap, 1), jnp.int32),
            jax.ShapeDtypeStruct((e, cap, 1), F32),
        ],
        compiler_params=_cparams(("parallel",)),
    )(aff_t.reshape(e, nc, LANES))


def _row_copy(src_hbm, tok, dst_ref, r, sem):
    return pltpu.make_async_copy(src_hbm.at[pl.ds(tok, 1), :], dst_ref.at[pl.ds(r, 1), :], sem)


def _moe_up_body(idx_ref, h_hbm, gate_ref, wg_ref, wu_ref, hid_ref, xe_ref, stage_ref, sem, *, cap, rows,
                 row_chunk):
    e = pl.program_id(0)
    f = pl.program_id(1)

    @pl.when(f == 0)
    def _():
        nchunk = cap // rows

        def start(c, slot):
            for r in range(rows):
                tok = idx_ref[e * cap + c * rows + r]
                _row_copy(h_hbm, tok, stage_ref.at[slot], r, sem.at[slot]).start()

        def wait(slot):
            pltpu.make_async_copy(h_hbm.at[pl.ds(0, rows), :], stage_ref.at[slot], sem.at[slot]).wait()

        start(0, 0)
        for c in range(nchunk):
            slot = c % 2
            if c + 1 < nchunk:
                start(c + 1, 1 - slot)
            wait(slot)
            words = stage_ref[slot]
            half = D_MODEL // 2
            xe_ref[c * rows:(c + 1) * rows, :half] = pltpu.bitcast(words << 16, F32).astype(BF16)
            xe_ref[c * rows:(c + 1) * rows, half:] = pltpu.bitcast(
                words & jnp.uint32(0xFFFF0000), F32).astype(BF16)

    wg = wg_ref[...].astype(BF16)
    wu = wu_ref[...].astype(BF16)
    for c in range(cap // row_chunk):
        rs = slice(c * row_chunk, (c + 1) * row_chunk)
        xe = xe_ref[rs, :]
        g = _dot(xe, wg)
        u = _dot(xe, wu)
        hid_ref[rs, :] = (g / (1.0 + jnp.exp(-g)) * u * gate_ref[rs, :]).astype(BF16)


def _moe_up(idx_flat, h2, gate, w_gate, w_up, cap):
    n = h2.shape[0]
    tf = TF_MOE
    rows = min(G_GATHER, cap)
    grid_spec = pltpu.PrefetchScalarGridSpec(
        num_scalar_prefetch=1,
        grid=(N_EXPERTS, EXPERT_FF // tf),
        in_specs=[
            pl.BlockSpec(memory_space=pl.ANY),
            pl.BlockSpec((None, cap, 1), lambda e, f, idx: (e, 0, 0)),
            pl.BlockSpec((None, D_MODEL, tf), lambda e, f, idx: (e, 0, f)),
            pl.BlockSpec((None, D_MODEL, tf), lambda e, f, idx: (e, 0, f)),
        ],
        out_specs=pl.BlockSpec((None, cap, tf), lambda e, f, idx: (e, 0, f)),
        scratch_shapes=[
            pltpu.VMEM((cap, D_MODEL), BF16),
            pltpu.VMEM((2, rows, D_MODEL // 2), jnp.uint32),
            pltpu.SemaphoreType.DMA((2,)),
        ],
    )
    return pl.pallas_call(
        functools.partial(_moe_up_body, cap=cap, rows=rows, row_chunk=min(ROW_CHUNK_UP, cap)),
        grid_spec=grid_spec,
        out_shape=jax.ShapeDtypeStruct((N_EXPERTS, cap, EXPERT_FF), BF16),
        compiler_params=_cparams(("arbitrary", "arbitrary")),
    )(idx_flat, h2, gate, w_gate, w_up)


def _moe_down_body(idx_ref, hid_ref, wd_ref, x_hbm, out_hbm, acc_ref, res_ref, sem, *, cap, rows, nkt):
    del x_hbm
    e = pl.program_id(0)
    rb = pl.program_id(1)
    kt = pl.program_id(2)
    last_kt = pl.num_programs(2) - 1
    base = e * cap + rb * rows
    gather_sem, scatter_sem = sem.at[0], sem.at[1]

    def wait_scatter():
        pltpu.make_async_copy(res_ref, out_hbm.at[pl.ds(0, rows), :], scatter_sem).wait()

    @pl.when(kt == 0)
    def _():
        @pl.when((e > 0) | (rb > 0))
        def _():
            wait_scatter()

    seg = rows // nkt
    for u in range(seg):
        r = kt * seg + u
        pltpu.make_async_copy(out_hbm.at[pl.ds(idx_ref[base + r], 1), :], res_ref.at[pl.ds(r, 1), :],
                              gather_sem).start()

    def partial_product():
        return _dot(hid_ref[...], wd_ref[...].astype(BF16))

    @pl.when(kt == 0)
    def _():
        acc_ref[...] = partial_product()

    @pl.when((kt > 0) & (kt < last_kt))
    def _():
        acc_ref[...] += partial_product()

    @pl.when(kt == last_kt)
    def _():
        pltpu.make_async_copy(out_hbm.at[pl.ds(0, rows), :], res_ref, gather_sem).wait()
        res_ref[...] += acc_ref[...] + partial_product()

        def start(r, carry):
            pltpu.make_async_copy(res_ref.at[pl.ds(r, 1), :], out_hbm.at[pl.ds(idx_ref[base + r], 1), :],
                                  scatter_sem).start()
            return carry
        lax.fori_loop(0, rows, start, 0, unroll=DMA_UNROLL)

        @pl.when((e == pl.num_programs(0) - 1) & (rb == pl.num_programs(1) - 1))
        def _():
            wait_scatter()


def _moe_down(idx_flat, hid, w_down, x2, cap):
    n = x2.shape[0]
    rows = min(R_DOWN, cap)
    tk = TK_DOWN
    assert EXPERT_FF // tk >= 2
    grid_spec = pltpu.PrefetchScalarGridSpec(
        num_scalar_prefetch=1,
        grid=(N_EXPERTS, cap // rows, EXPERT_FF // tk),
        in_specs=[
            pl.BlockSpec((None, rows, tk), lambda e, rb, kt, idx: (e, rb, kt)),
            pl.BlockSpec((None, tk, D_MODEL), lambda e, rb, kt, idx: (e, kt, 0)),
            pl.BlockSpec(memory_space=pl.ANY),
        ],
        out_specs=pl.BlockSpec(memory_space=pl.ANY),
        scratch_shapes=[
            pltpu.VMEM((rows, D_MODEL), F32),
            pltpu.VMEM((rows, D_MODEL), F32),
            pltpu.SemaphoreType.DMA((2,)),
        ],
    )
    return pl.pallas_call(
        functools.partial(_moe_down_body, cap=cap, rows=rows, nkt=EXPERT_FF // tk),
        grid_spec=grid_spec,
        out_shape=jax.ShapeDtypeStruct((n, D_MODEL), F32),
        input_output_aliases={3: 0},
        compiler_params=_cparams(("arbitrary", "arbitrary", "arbitrary")),
    )(idx_flat, hid, w_down, x2)


def _pad_rope_cols(t):
    half = QK_ROPE // 2
    z = jnp.zeros(t.shape[:-1] + (half,), t.dtype)
    return jnp.concatenate([t[..., :half], z, t[..., half:], z], axis=-1)


def _rope_tables_padded(seq):
    inv = ROPE_THETA ** (-jnp.arange(0, QK_ROPE, 2, dtype=F32) / QK_ROPE)
    ang = jnp.arange(seq, dtype=F32)[:, None] * inv[None, :]
    cos, sin = jnp.cos(ang), jnp.sin(ang)
    z = jnp.zeros_like(cos)
    return jnp.concatenate([cos, z, cos, z], axis=-1), jnp.concatenate([-sin, z, sin, z], axis=-1)


def _prepare_weights(w_in, mla_w_q_b, mla_w_kv_b, w_o, w_router, mla_q_rope_norm_g, mla_k_rope_norm_g):
    qkv_cols = 3 * A_WIDTH
    w_a = w_in[:, :qkv_cols].astype(BF16)
    w_ql = w_in[:, qkv_cols:qkv_cols + Q_LORA].astype(BF16)
    w_kvl = w_in[:, qkv_cols + Q_LORA:qkv_cols + Q_LORA + KV_LORA].astype(BF16)
    w_kr = _pad_rope_cols(w_in[:, qkv_cols + Q_LORA + KV_LORA:]).astype(BF16)
    wq = mla_w_q_b.reshape(Q_LORA, MLA_HEADS, QK_NOPE + QK_ROPE)
    wq_pad = jnp.concatenate([wq[..., :QK_NOPE], _pad_rope_cols(wq[..., QK_NOPE:])], axis=-1)
    wq_pad = wq_pad.reshape(Q_LORA, MLA_HEADS * MLA_HEAD_PAD).astype(BF16)
    wr = jnp.pad(w_router, ((0, 0), (0, LANES - N_EXPERTS)))
    wr1 = wr.astype(BF16)
    wr2 = (wr - wr1.astype(F32)).astype(BF16)
    return dict(
        w_a=w_a, w_ql=w_ql, w_kvl=w_kvl, w_kr=w_kr, wq_pad=wq_pad,
        wkv=mla_w_kv_b.astype(BF16), w_o=w_o.astype(BF16), wr1=wr1, wr2=wr2,
        gqr_pad=_pad_rope_cols(mla_q_rope_norm_g)[None, :], gkr_pad=_pad_rope_cols(mla_k_rope_norm_g)[None, :],
    )


def _encoder_layer(x3, pw, norm_mix_g, a_q_norm_g, a_k_norm_g, mla_q_a_norm_g, mla_kv_a_norm_g,
                   mla_q_nope_norm_g, mla_k_nope_norm_g, out_norm_a_g, out_norm_b_g, norm_ffn_g,
                   w_gate, w_up, w_down):
    b, s, d = x3.shape
    n = b * s
    x = x3.reshape(n, d)
    row = lambda v: v[None, :]
    cos_pad, sin_pad = _rope_tables_padded(s)

    qkv_by_dilation = _inproj_a(x, row(norm_mix_g), pw["w_a"], row(a_q_norm_g), row(a_k_norm_g), b, s)
    branches = [_dilated_branch(qkv) for qkv in qkv_by_dilation]

    cq, ckv, kr = _inproj_lat(x, row(norm_mix_g), pw["w_ql"], pw["w_kvl"], pw["w_kr"], row(mla_q_a_norm_g),
                              row(mla_kv_a_norm_g), pw["gkr_pad"], cos_pad, sin_pad, s)
    qm, km, vm = _mla_up(cq, ckv, kr, pw["wq_pad"], pw["wkv"], row(mla_q_nope_norm_g), pw["gqr_pad"],
                         row(mla_k_nope_norm_g), cos_pad, sin_pad, s)
    wide = MLA_HEADS * MLA_HEAD_PAD
    ob = _mla_flash(qm.reshape(b, s, wide), km.reshape(b, s, wide), vm.reshape(b, s, wide))

    x2 = _outproj(branches, ob.reshape(n, MLA_WIDTH), row(out_norm_a_g), row(out_norm_b_g), pw["w_o"], x, s)

    h2, aff_t = _router(x2, row(norm_ffn_g), pw["wr1"], pw["wr2"])
    cap = CAPACITY_FACTOR * n // N_EXPERTS
    idx, gate = _topk(aff_t, cap)
    idx_flat = idx.reshape(N_EXPERTS * cap)
    hid = _moe_up(idx_flat, h2, gate, w_gate, w_up, cap)
    y = _moe_down(idx_flat, hid, w_down, x2, cap)
    return y.reshape(b, s, d)


def kernel(x_prompt, x_sample, norm_mix_g, w_in, a_q_norm_g, a_k_norm_g, mla_q_a_norm_g, mla_w_q_b, mla_kv_a_norm_g, mla_w_kv_b, mla_q_nope_norm_g, mla_q_rope_norm_g, mla_k_nope_norm_g, mla_k_rope_norm_g, out_norm_a_g, out_norm_b_g, w_o, norm_ffn_g, w_router, w_gate, w_up, w_down):
    depth = w_in.shape[0]
    prepared = [_prepare_weights(w_in[l], mla_w_q_b[l], mla_w_kv_b[l], w_o[l], w_router[l],
                                 mla_q_rope_norm_g[l], mla_k_rope_norm_g[l]) for l in range(depth)]

    def run_trunk(x):
        for l in range(depth):
            pw = prepared[l]
            x = _encoder_layer(x, pw, norm_mix_g[l], a_q_norm_g[l], a_k_norm_g[l], mla_q_a_norm_g[l],
                               mla_kv_a_norm_g[l], mla_q_nope_norm_g[l], mla_k_nope_norm_g[l],
                               out_norm_a_g[l], out_norm_b_g[l], norm_ffn_g[l], w_gate[l], w_up[l], w_down[l])
        return x

    return run_trunk(x_prompt), run_trunk(x_sample)
```
